```python
import jax, jax.numpy as jnp
from jax import lax
import numpy as np

D_MODEL = 1024
BATCH = 2
SEQ = 8192
DEPTH = 2

HEAD_DIM = 64
A_HEADS = D_MODEL // 128
B_HEADS = D_MODEL // 256
C_HEADS = D_MODEL // 256
A_DIM = A_HEADS * HEAD_DIM
B_DIM = B_HEADS * HEAD_DIM
C_DIM = C_HEADS * HEAD_DIM
D_MIX = A_DIM + B_DIM + C_DIM
IN_SPLITS = (A_DIM, A_DIM, A_DIM, A_HEADS, 2 * B_DIM, B_DIM, B_HEADS, B_HEADS, B_DIM, C_DIM, C_DIM, C_DIM, C_DIM)
IN_WIDTH = 3 * A_DIM + A_HEADS + 4 * B_DIM + 2 * B_HEADS + 4 * C_DIM
Q_BLOCK = 128
MLSTM_CHUNK = 64
HGRN_CHUNK = 16
MLSTM_CONV = 4
FFN_CONV = 3
D_FF = ((8 * D_MODEL // 3 + 255) // 256) * 256
EPS = 1e-6

kernel_name = "hymba_fox_mlstm_hgrn2_trunk"


def _rms(x, g):
    xf = x.astype(jnp.float32)
    y = xf * lax.rsqrt(jnp.mean(xf * xf, axis=-1, keepdims=True) + EPS)
    return (y * g.astype(jnp.float32)).astype(x.dtype)


def _heads(z, h):
    b, t, _ = z.shape
    return z.reshape(b, t, h, -1).transpose(0, 2, 1, 3).astype(jnp.float32)


def _merge(o):
    return o.transpose(0, 2, 1, 3)


def _split_cols(z):
    idx = np.cumsum(IN_SPLITS)[:-1].tolist()
    return jnp.split(z, idx, axis=-1)


def _causal_dwconv(x, w):
    k, c = w.shape
    return lax.conv_general_dilated(x, w[:, None, :], window_strides=(1,), padding=[(k - 1, 0)],
                                    dimension_numbers=('NWC', 'WIO', 'NWC'), feature_group_count=c)


def _fox_attention(q, k, v, c):
    bsz, h, t, d = q.shape
    nb = t // Q_BLOCK
    qb = q.reshape(bsz, h, nb, Q_BLOCK, d).transpose(2, 0, 1, 3, 4)
    cb = c.reshape(bsz, h, nb, Q_BLOCK).transpose(2, 0, 1, 3)
    kpos = jnp.arange(t)
    scale = d ** -0.5

    def one_block(args):
        qi, ci, bi = args
        s = jnp.einsum('bhqd,bhkd->bhqk', qi, k) * scale
        s = s + ci[..., None] - c[:, :, None, :]
        qpos = bi * Q_BLOCK + jnp.arange(Q_BLOCK)
        s = jnp.where(kpos[None, :] <= qpos[:, None], s, -jnp.inf)
        p = jax.nn.softmax(s, axis=-1)
        return jnp.einsum('bhqk,bhkd->bhqd', p, v)

    ob = lax.map(one_block, (qb, cb, jnp.arange(nb)))
    return ob.transpose(1, 2, 0, 3, 4).reshape(bsz, h, t, d)


def _mlstm_chunkwise(q, k, v, i_pre, f_pre):
    bsz, h, t, d = q.shape
    L = MLSTM_CHUNK
    nc = t // L
    k = k * d ** -0.5
    qc = q.reshape(bsz, h, nc, L, d)
    kc = k.reshape(bsz, h, nc, L, d)
    vc = v.reshape(bsz, h, nc, L, d)
    ic = i_pre.reshape(bsz, h, nc, L)
    bc = jnp.cumsum(jax.nn.log_sigmoid(f_pre).reshape(bsz, h, nc, L), axis=-1)
    gc = bc[..., -1]
    ac = gc[..., None] - bc + ic

    def step(carry, inp):
        cs, ns, m = carry
        kj, vj, aj, gj = inp
        m_new = jnp.maximum(gj + m, jnp.max(aj, axis=-1))
        decay = jnp.exp(gj + m - m_new)
        w = jnp.exp(aj - m_new[..., None])
        cs_new = decay[..., None, None] * cs + jnp.einsum('bhl,bhld,bhle->bhde', w, kj, vj)
        ns_new = decay[..., None] * ns + jnp.einsum('bhl,bhld->bhd', w, kj)
        return (cs_new, ns_new, m_new), (cs, ns, m)

    mv = lambda a: jnp.moveaxis(a, 2, 0)
    init = (jnp.zeros((bsz, h, d, d), jnp.float32), jnp.zeros((bsz, h, d), jnp.float32),
            jnp.full((bsz, h), -jnp.inf, jnp.float32))
    _, (c_prev, n_prev, m_prev) = lax.scan(step, init, (mv(kc), mv(vc), mv(ac), gc.transpose(2, 0, 1)))
    c_prev = jnp.moveaxis(c_prev, 0, 2)
    n_prev = jnp.moveaxis(n_prev, 0, 2)
    m_prev = jnp.moveaxis(m_prev, 0, 2)

    tri = jnp.tril(jnp.ones((L, L), bool))
    log_d = jnp.where(tri, bc[..., :, None] - bc[..., None, :] + ic[..., None, :], -jnp.inf)
    m_inter = bc + m_prev[..., None]
    m_out = jnp.maximum(m_inter, jnp.max(log_d, axis=-1))
    sqk = jnp.einsum('bhcld,bhcsd->bhcls', qc, kc) * jnp.exp(log_d - m_out[..., None])
    inter_w = jnp.exp(m_inter - m_out)
    num = jnp.einsum('bhcls,bhcse->bhcle', sqk, vc) + inter_w[..., None] * jnp.einsum('bhcld,bhcde->bhcle', qc, c_prev)
    den = jnp.sum(sqk, axis=-1) + inter_w * jnp.einsum('bhcld,bhcd->bhcl', qc, n_prev)
    out = num / jnp.maximum(jnp.abs(den), jnp.exp(-m_out))[..., None]
    return out.reshape(bsz, h, t, d)


def _hgrn2_chunkwise(q, logf, i):
    bsz, h, t, dk = q.shape
    dv = i.shape[-1]
    L = HGRN_CHUNK
    nc = t // L
    k = -jnp.expm1(logf)
    qc = q.reshape(bsz, h, nc, L, dk)
    kc = k.reshape(bsz, h, nc, L, dk)
    vc = i.reshape(bsz, h, nc, L, dv)
    bc = jnp.cumsum(logf.reshape(bsz, h, nc, L, dk), axis=3)
    blast = bc[:, :, :, -1]
    tri = jnp.tril(jnp.ones((L, L), bool))
    rel = jnp.where(tri[:, :, None], bc[:, :, :, :, None, :] - bc[:, :, :, None, :, :], -jnp.inf)
    att = jnp.einsum('bhctd,bhcsd,bhctsd->bhcts', qc, kc, jnp.exp(rel))
    intra = jnp.einsum('bhcts,bhcse->bhcte', att, vc)

    def step(s, inp):
        kj, vj, bj, blj = inp
        kd = kj * jnp.exp(blj[:, :, None, :] - bj)
        return jnp.exp(blj)[..., None] * s + jnp.einsum('bhld,bhle->bhde', kd, vj), s

    mv = lambda a: jnp.moveaxis(a, 2, 0)
    _, s_prev = lax.scan(step, jnp.zeros((bsz, h, dk, dv), jnp.float32), (mv(kc), mv(vc), mv(bc), mv(blast)))
    s_prev = jnp.moveaxis(s_prev, 0, 2)
    inter = jnp.einsum('bhctd,bhcde->bhcte', qc * jnp.exp(bc), s_prev)
    return (intra + inter).reshape(bsz, h, t, dv)


def _mixer(h, lb, w_in, b_in, a_q_g, a_k_g, b_conv_w, out_g, w_out):
    bsz, t, _ = h.shape
    f32 = jnp.float32
    z = h @ w_in + b_in
    (a_q, a_k, a_v, a_f, b_qk, b_v, b_i, b_f, b_o, c_q, c_f, c_i, c_g) = _split_cols(z)
    qa = _rms(_heads(a_q, A_HEADS), a_q_g)
    ka = _rms(_heads(a_k, A_HEADS), a_k_g)
    ca = jnp.cumsum(jax.nn.log_sigmoid(a_f.astype(f32)), axis=1).transpose(0, 2, 1)
    oa = _fox_attention(qa, ka, _heads(a_v, A_HEADS), ca)
    qk = jax.nn.silu(_causal_dwconv(b_qk, b_conv_w))
    qb, kb = jnp.split(qk, 2, axis=-1)
    ob = _mlstm_chunkwise(_heads(qb, B_HEADS), _heads(kb, B_HEADS), _heads(b_v, B_HEADS),
                          b_i.astype(f32).transpose(0, 2, 1), b_f.astype(f32).transpose(0, 2, 1))
    lbh = lb.reshape(C_HEADS, 1, HEAD_DIM)
    logf = jnp.logaddexp(jnp.log(lbh), jnp.log1p(-lbh) + jax.nn.log_sigmoid(_heads(c_f, C_HEADS)))
    oc = _hgrn2_chunkwise(jax.nn.silu(_heads(c_q, C_HEADS)), logf, _heads(c_i, C_HEADS))
    g_a, g_b, g_c = jnp.split(out_g.astype(f32), [A_DIM, A_DIM + B_DIM])
    ya = _rms(_merge(oa), g_a.reshape(A_HEADS, HEAD_DIM)).reshape(bsz, t, A_DIM)
    yb = jax.nn.sigmoid(b_o.astype(f32)) * _rms(_merge(ob), g_b.reshape(B_HEADS, HEAD_DIM)).reshape(bsz, t, B_DIM)
    yc = jax.nn.silu(c_g.astype(f32)) * _rms(_merge(oc), g_c.reshape(C_HEADS, HEAD_DIM)).reshape(bsz, t, C_DIM)
    y = jnp.concatenate([ya, yb, yc], axis=-1).astype(h.dtype)
    return y @ w_out


def _conv_glu(h, w_up, conv_w, conv_b, w_down):
    u = _causal_dwconv(h @ w_up, conv_w) + conv_b
    gate, up = jnp.split(u, 2, axis=-1)
    return (jax.nn.silu(gate) * up) @ w_down


def setup_inputs(seed: int = 0) -> dict:
    key = jax.random.key(seed)
    ks = jax.random.split(key, 16)
    f32 = jnp.float32
    nrm = lambda k, shape, s: jax.random.normal(k, shape, f32) * s
    gate_offsets = jnp.concatenate([
        jnp.zeros((3 * A_DIM,), f32), jnp.linspace(1.0, 4.0, A_HEADS, dtype=f32),
        jnp.zeros((3 * B_DIM + B_HEADS,), f32), jnp.linspace(3.0, 6.0, B_HEADS, dtype=f32),
        jnp.zeros((B_DIM + 4 * C_DIM,), f32)])
    return {
        "x": nrm(ks[0], (BATCH, SEQ, D_MODEL), 1.0),
        "lb_logits": nrm(ks[1], (DEPTH, C_DIM), 0.5),
        "norm_mix_g": 1.0 + nrm(ks[2], (DEPTH, D_MODEL), 0.02),
        "w_in": nrm(ks[3], (DEPTH, D_MODEL, IN_WIDTH), D_MODEL ** -0.5),
        "b_in": nrm(ks[4], (DEPTH, IN_WIDTH), 0.02) + gate_offsets,
        "a_q_g": 1.0 + nrm(ks[5], (DEPTH, HEAD_DIM), 0.02),
        "a_k_g": 1.0 + nrm(ks[6], (DEPTH, HEAD_DIM), 0.02),
        "b_conv_w": nrm(ks[7], (DEPTH, MLSTM_CONV, 2 * B_DIM), MLSTM_CONV ** -0.5),
        "out_g": 1.0 + nrm(ks[8], (DEPTH, D_MIX), 0.02),
        "w_out": nrm(ks[9], (DEPTH, D_MIX, D_MODEL), D_MIX ** -0.5),
        "norm_ffn_g": 1.0 + nrm(ks[10], (DEPTH, D_MODEL), 0.02),
        "w_up": nrm(ks[11], (DEPTH, D_MODEL, 2 * D_FF), D_MODEL ** -0.5),
        "ffn_conv_w": nrm(ks[12], (DEPTH, FFN_CONV, 2 * D_FF), FFN_CONV ** -0.5),
        "ffn_conv_b": nrm(ks[13], (DEPTH, 2 * D_FF), 0.02),
        "w_down": nrm(ks[14], (DEPTH, D_FF, D_MODEL), D_FF ** -0.5),
    }


def reference(x, lb_logits, norm_mix_g, w_in, b_in, a_q_g, a_k_g, b_conv_w, out_g, w_out,
              norm_ffn_g, w_up, ffn_conv_w, ffn_conv_b, w_down):
    p = jax.nn.softmax(lb_logits.astype(jnp.float32), axis=0)
    lb_all = jnp.maximum(jnp.cumsum(p, axis=0) - p[0], 0.0)
    for l in range(DEPTH):
        h = _rms(x, norm_mix_g[l])
        x = x + _mixer(h, lb_all[l], w_in[l], b_in[l], a_q_g[l], a_k_g[l], b_conv_w[l], out_g[l], w_out[l])
        h = _rms(x, norm_ffn_g[l])
        x = x + _conv_glu(h, w_up[l], ffn_conv_w[l], ffn_conv_b[l], w_down[l])
    return x
```

```python
import functools

import numpy as np
import jax
import jax.numpy as jnp
from jax import lax
from jax.experimental import pallas as pl
from jax.experimental.pallas import tpu as pltpu

F32 = jnp.float32
BF16 = jnp.bfloat16
HIGHEST = lax.Precision.HIGHEST
NEG_INF = float("-inf")
LOG2E = 1.4426950408889634

D_MODEL = 1024
HEAD_DIM = 64
A_HEADS, B_HEADS, C_HEADS = 8, 4, 4
A_DIM, B_DIM, C_DIM = A_HEADS * HEAD_DIM, B_HEADS * HEAD_DIM, C_HEADS * HEAD_DIM
D_MIX = A_DIM + B_DIM + C_DIM
D_FF = 2816
MLSTM_CONV = 4
FFN_CONV = 3
EPS = 1e-6

LANE = 128
BF16_ROWS = 16
VMEM_LIMIT = 56 * 1024 * 1024

QKV_W = 3 * A_DIM
REST_W = 10 * B_DIM
IN_W = QKV_W + REST_W + LANE

TM_PROJ = 512
TQ_FOX = 512
TC_MLSTM = 512
L_MLSTM = 128
TC_HGRN = 256
L_HGRN = 16
FF_CHUNK = 256


def _reorder_in_cols(w):
    a_f = 3 * A_DIM
    b_qk = a_f + A_HEADS
    b_v = b_qk + 2 * B_DIM
    b_i = b_v + B_DIM
    b_f = b_i + B_HEADS
    b_o = b_f + B_HEADS
    c_q = b_o + B_DIM
    end = c_q + 4 * C_DIM
    sl = lambda lo, hi: lax.slice_in_dim(w, lo, hi, axis=-1)
    n_gate = A_HEADS + 2 * B_HEADS
    pad = jnp.zeros(w.shape[:-1] + (LANE - n_gate,), w.dtype)
    out = jnp.concatenate([
        sl(0, a_f), sl(b_qk, b_i), sl(b_o, end),
        jnp.repeat(sl(b_i, b_f), HEAD_DIM, axis=-1), jnp.repeat(sl(b_f, b_o), HEAD_DIM, axis=-1),
        sl(a_f, b_qk), sl(b_i, b_o), pad], axis=-1)
    assert out.shape[-1] == IN_W
    return out


def _log_sigmoid(x):
    return jnp.minimum(x, 0.0) - jnp.log1p(jnp.exp(-jnp.abs(x)))


def _sigmoid(x):
    return 1.0 / (1.0 + jnp.exp(-x))


def _head_of_lane(shape):
    return lax.broadcasted_iota(jnp.int32, shape, len(shape) - 1) // HEAD_DIM


def _expand_heads(cols, head):
    out = cols[-1]
    for h in range(len(cols) - 2, -1, -1):
        out = jnp.where(head == h, cols[h], out)
    return out


def _group_mean_sq(z, grp_ref):
    w = z.shape[-1]
    zz = (z * z).astype(BF16)
    return jnp.dot(zz, grp_ref[0:w, 0:w], preferred_element_type=F32) * (1.0 / HEAD_DIM)


def _inproj_kernel(x_ref, g_ref, w_ref, b_ref, gq_ref, gk_ref, grp_ref, qkv_ref, rest_ref, gates_ref):
    x = x_ref[...]
    ms = jnp.mean(x * x, axis=-1, keepdims=True)
    h = (x * lax.rsqrt(ms + EPS) * g_ref[...]).astype(BF16)

    def proj(lo, hi):
        return jnp.dot(h, w_ref[:, lo:hi], preferred_element_type=F32) + b_ref[:, lo:hi]

    q = proj(0, A_DIM)
    qkv_ref[:, 0:A_DIM] = (q * lax.rsqrt(_group_mean_sq(q, grp_ref) + EPS) * gq_ref[...]).astype(BF16)
    k = proj(A_DIM, 2 * A_DIM)
    qkv_ref[:, A_DIM:2 * A_DIM] = (k * lax.rsqrt(_group_mean_sq(k, grp_ref) + EPS) * gk_ref[...]).astype(BF16)
    qkv_ref[:, 2 * A_DIM:QKV_W] = proj(2 * A_DIM, QKV_W).astype(BF16)
    for c in range(REST_W // A_DIM):
        rest_ref[:, c * A_DIM:(c + 1) * A_DIM] = proj(QKV_W + c * A_DIM, QKV_W + (c + 1) * A_DIM)
    gates_ref[...] = proj(QKV_W + REST_W, IN_W)


def _inproj(x2, g, w, b, gq, gk, grp):
    n = x2.shape[0]
    const = lambda i: (0, 0)
    return pl.pallas_call(
        _inproj_kernel,
        grid=(n // TM_PROJ,),
        in_specs=[
            pl.BlockSpec((TM_PROJ, D_MODEL), lambda i: (i, 0)),
            pl.BlockSpec((1, D_MODEL), const),
            pl.BlockSpec((D_MODEL, IN_W), const),
            pl.BlockSpec((1, IN_W), const),
            pl.BlockSpec((1, A_DIM), const),
            pl.BlockSpec((1, A_DIM), const),
            pl.BlockSpec((A_DIM, A_DIM), const),
        ],
        out_specs=[
            pl.BlockSpec((TM_PROJ, QKV_W), lambda i: (i, 0)),
            pl.BlockSpec((TM_PROJ, REST_W), lambda i: (i, 0)),
            pl.BlockSpec((TM_PROJ, LANE), lambda i: (i, 0)),
        ],
        out_shape=[
            jax.ShapeDtypeStruct((n, QKV_W), BF16),
            jax.ShapeDtypeStruct((n, REST_W), F32),
            jax.ShapeDtypeStruct((n, LANE), F32),
        ],
        compiler_params=pltpu.CompilerParams(dimension_semantics=("arbitrary",), vmem_limit_bytes=VMEM_LIMIT),
        name="inproj",
    )(x2, g, w, b, gq, gk, grp)


def _gates_kernel(g_ref, tri_ref, c_ref, m_ref, *, t):
    tri = tri_ref[...]
    row = lax.broadcasted_iota(jnp.int32, (8, L_MLSTM), 0)

    def body(j, carry):
        off = pl.multiple_of(j * L_MLSTM, L_MLSTM)
        blk = g_ref[0, :, pl.ds(off, L_MLSTM)]
        cs = jnp.dot(_log_sigmoid(blk), tri, precision=HIGHEST, preferred_element_type=F32)
        c_ref[0, :, pl.ds(off, L_MLSTM)] = (cs[0:8] + carry) * LOG2E
        m_ref[0, :, pl.ds(off, L_MLSTM)] = jnp.where(row < B_HEADS, blk[8:16], cs[8:16])
        return carry + cs[0:8, L_MLSTM - 1:L_MLSTM]

    lax.fori_loop(0, t // L_MLSTM, body, jnp.zeros((8, 1), F32))


def _gates(g_t, tri_u):
    bsz, _, t = g_t.shape
    return pl.pallas_call(
        functools.partial(_gates_kernel, t=t),
        grid=(bsz,),
        in_specs=[
            pl.BlockSpec((1, 16, t), lambda b: (b, 0, 0)),
            pl.BlockSpec((L_MLSTM, L_MLSTM), lambda b: (0, 0)),
        ],
        out_specs=[
            pl.BlockSpec((1, 8, t), lambda b: (b, 0, 0)),
            pl.BlockSpec((1, 8, t), lambda b: (b, 0, 0)),
        ],
        out_shape=[jax.ShapeDtypeStruct((bsz, 8, t), F32), jax.ShapeDtypeStruct((bsz, 8, t), F32)],
        compiler_params=pltpu.CompilerParams(dimension_semantics=("arbitrary",)),
        name="gates",
    )(g_t, tri_u)


def _fox_kernel(q_ref, k_ref, v_ref, c_ref, o_ref, *, tq):
    i = pl.program_id(2)
    q = q_ref[...]
    lane = lax.broadcasted_iota(jnp.int32, (tq, LANE), 1)
    zero = jnp.zeros_like(q)
    qh = (jnp.where(lane < HEAD_DIM, q, zero), jnp.where(lane >= HEAD_DIM, q, zero))
    tri = lax.broadcasted_iota(jnp.int32, (tq, tq), 1) <= lax.broadcasted_iota(jnp.int32, (tq, tq), 0)

    def step(j, carry, masked):
        off = pl.multiple_of(j * tq, tq)
        kt = k_ref[pl.ds(off, tq), :]
        vt = v_ref[pl.ds(off, tq), :]
        new = []
        for hh in range(2):
            m, l, acc = carry[hh]
            s = lax.dot_general(qh[hh], kt, (((1,), (1,)), ((), ())), preferred_element_type=F32)
            s = s - c_ref[0, hh:hh + 1, pl.ds(off, tq)]
            if masked:
                s = jnp.where(tri, s, NEG_INF)
            m_new = jnp.maximum(m, jnp.max(s, axis=-1, keepdims=True))
            alpha = jnp.exp2(m - m_new)
            p = jnp.exp2(s - m_new)
            l = alpha * l + jnp.sum(p, axis=-1, keepdims=True)
            acc = alpha * acc + jnp.dot(p.astype(BF16), vt, preferred_element_type=F32)
            new.append((m_new, l, acc))
        return tuple(new)

    init = tuple((jnp.full((tq, 1), NEG_INF, F32), jnp.zeros((tq, 1), F32), jnp.zeros((tq, LANE), F32))
                 for _ in range(2))
    carry = lax.fori_loop(0, i, lambda j, c: step(j, c, False), init)
    (_, l0, a0), (_, l1, a1) = step(i, carry, True)
    o_ref[...] = jnp.where(lane < HEAD_DIM, a0 / l0, a1 / l1)


def _fox(qkv, c_t, bsz, t):
    n = qkv.shape[0]
    nq = t // TQ_FOX
    pairs = A_HEADS // 2
    return pl.pallas_call(
        functools.partial(_fox_kernel, tq=TQ_FOX),
        grid=(bsz, pairs, nq),
        in_specs=[
            pl.BlockSpec((TQ_FOX, LANE), lambda b, p, i: (b * nq + i, p)),
            pl.BlockSpec((t, LANE), lambda b, p, i: (b, pairs + p)),
            pl.BlockSpec((t, LANE), lambda b, p, i: (b, 2 * pairs + p)),
            pl.BlockSpec((1, 2, t), lambda b, p, i: (b * pairs + p, 0, 0)),
        ],
        out_specs=pl.BlockSpec((TQ_FOX, LANE), lambda b, p, i: (b * nq + i, p)),
        out_shape=jax.ShapeDtypeStruct((n, A_DIM), F32),
        compiler_params=pltpu.CompilerParams(
            dimension_semantics=("arbitrary", "arbitrary", "arbitrary"), vmem_limit_bytes=VMEM_LIMIT),
        name="fox",
    )(qkv, qkv, qkv, c_t)


def _mlstm_kernel(qk_ref, halo_ref, v_ref, bi_ref, bf_ref, gt_ref, cw_ref, tri_ref, grp_ref, o_ref,
                  buf_ref, c_st, n_st, m_st, *, tc, chunk):
    j = pl.program_id(1)
    width = B_DIM

    @pl.when(j == 0)
    def _():
        c_st[...] = jnp.zeros_like(c_st)
        n_st[...] = jnp.zeros_like(n_st)
        m_st[...] = jnp.full(m_st.shape, NEG_INF, F32)

    buf_ref[0:8, :] = jnp.where(j == 0, 0.0, halo_ref[...])
    buf_ref[8:8 + tc, :] = qk_ref[...]
    cw = cw_ref[...]
    y = cw[MLSTM_CONV - 1:MLSTM_CONV] * buf_ref[8:8 + tc, :]
    for d in range(1, MLSTM_CONV):
        y = y + cw[MLSTM_CONV - 1 - d:MLSTM_CONV - d] * buf_ref[8 - d:8 - d + tc, :]
    y = y * _sigmoid(y)
    q_all = y[:, :width]
    k_all = y[:, width:] * (HEAD_DIM ** -0.5)

    head = _head_of_lane((chunk, width))
    head_sq = _head_of_lane((width, width))
    blockdiag = head_sq == lax.broadcasted_iota(jnp.int32, (width, width), 0) // HEAD_DIM
    causal = lax.broadcasted_iota(jnp.int32, (chunk, chunk), 1) <= lax.broadcasted_iota(jnp.int32, (chunk, chunk), 0)
    tri = tri_ref[...]
    grp = grp_ref[...]

    for c in range(tc // chunk):
        r0 = c * chunk
        q = q_all[r0:r0 + chunk]
        k = k_all[r0:r0 + chunk]
        v = v_ref[r0:r0 + chunk, :]
        qb, kb, vb = q.astype(BF16), k.astype(BF16), v.astype(BF16)
        i_x = bi_ref[r0:r0 + chunk, :]
        bc_x = jnp.dot(tri, _log_sigmoid(bf_ref[r0:r0 + chunk, :]), precision=HIGHEST,
                       preferred_element_type=F32)
        c_prev, n_prev, m_prev = c_st[...], n_st[...], m_st[...]

        m_cols, den_cols = [], []
        num_x = jnp.zeros((chunk, width), F32)
        for h in range(B_HEADS):
            lo = h * HEAD_DIM
            bc_col = bc_x[:, lo:lo + 1]
            row = gt_ref[0, h:h + 1, r0:r0 + chunk] - gt_ref[0, B_HEADS + h:B_HEADS + h + 1, r0:r0 + chunk]
            log_d = jnp.where(causal, bc_col + row, NEG_INF)
            m_out = jnp.maximum(bc_col + m_prev[:, lo:lo + 1], jnp.max(log_d, axis=-1, keepdims=True))
            s = lax.dot_general(jnp.where(head == h, qb, jnp.zeros_like(qb)), kb, (((1,), (1,)), ((), ())),
                                preferred_element_type=F32)
            sqk = s * jnp.exp(log_d - m_out)
            num_x = jnp.where(head == h, jnp.dot(sqk.astype(BF16), vb, preferred_element_type=F32), num_x)
            m_cols.append(m_out)
            den_cols.append(jnp.sum(sqk, axis=-1, keepdims=True))
        m_out_x = _expand_heads(m_cols, head)
        den_x = _expand_heads(den_cols, head)

        inter_w = jnp.exp(bc_x + m_prev - m_out_x)
        q_c = jnp.dot(qb, c_prev.astype(BF16), preferred_element_type=F32)
        q_n = jnp.dot(q * n_prev, grp, precision=HIGHEST, preferred_element_type=F32)
        num = num_x + inter_w * q_c
        den = den_x + inter_w * q_n
        o_ref[r0:r0 + chunk, :] = num / jnp.maximum(jnp.abs(den), jnp.exp(-m_out_x))

        g_x = bc_x[chunk - 1:chunk, :]
        a_x = g_x - bc_x + i_x
        m_new = jnp.maximum(g_x + m_prev, jnp.max(a_x, axis=0, keepdims=True))
        decay = jnp.exp(g_x + m_prev - m_new)
        kw = k * jnp.exp(a_x - m_new)
        upd = lax.dot_general(kw.astype(BF16), vb, (((0,), (0,)), ((), ())), preferred_element_type=F32)
        c_st[...] = decay * c_prev + jnp.where(blockdiag, upd, 0.0)
        n_st[...] = decay * n_prev + jnp.sum(kw, axis=0, keepdims=True)
        m_st[...] = m_new


def _mlstm(rest, g_tm, conv_w, tri_l, grp, bsz, t):
    n = rest.shape[0]
    tc, nt = TC_MLSTM, t // TC_MLSTM
    blk = lambda col: pl.BlockSpec((tc, B_DIM), lambda b, j: (b * nt + j, col))
    const = lambda b, j: (0, 0)
    return pl.pallas_call(
        functools.partial(_mlstm_kernel, tc=tc, chunk=L_MLSTM),
        grid=(bsz, nt),
        in_specs=[
            pl.BlockSpec((tc, 2 * B_DIM), lambda b, j: (b * nt + j, 0)),
            pl.BlockSpec((8, 2 * B_DIM), lambda b, j: (jnp.maximum((b * nt + j) * (tc // 8) - 1, 0), 0)),
            blk(2), blk(8), blk(9),
            pl.BlockSpec((1, 8, tc), lambda b, j: (b, 0, j)),
            pl.BlockSpec((MLSTM_CONV, 2 * B_DIM), const),
            pl.BlockSpec((L_MLSTM, L_MLSTM), const),
            pl.BlockSpec((B_DIM, B_DIM), const),
        ],
        out_specs=pl.BlockSpec((tc, B_DIM), lambda b, j: (b * nt + j, 0)),
        out_shape=jax.ShapeDtypeStruct((n, B_DIM), F32),
        scratch_shapes=[
            pltpu.VMEM((tc + 8, 2 * B_DIM), F32),
            pltpu.VMEM((B_DIM, B_DIM), F32),
            pltpu.VMEM((1, B_DIM), F32),
            pltpu.VMEM((1, B_DIM), F32),
        ],
        compiler_params=pltpu.CompilerParams(dimension_semantics=("arbitrary", "arbitrary"),
                                             vmem_limit_bytes=VMEM_LIMIT),
        name="mlstm",
    )(rest, rest, rest, rest, rest, g_tm, conv_w, tri_l, grp)


def _hgrn_kernel(q_ref, f_ref, i_ref, la_ref, l1m_ref, tri_ref, tot_ref, o_ref, st_ref, *, tc, chunk):
    j = pl.program_id(1)
    width = C_DIM

    @pl.when(j == 0)
    def _():
        st_ref[...] = jnp.zeros_like(st_ref)

    xq = q_ref[...]
    q = xq * _sigmoid(xq)
    xf = f_ref[...]
    ls = _log_sigmoid(xf)
    a = la_ref[...]
    b = l1m_ref[...] + ls
    logf = jnp.maximum(a, b) + jnp.log1p(jnp.exp(-jnp.abs(a - b)))
    kk = jnp.exp(l1m_ref[...] + ls - xf)
    v = i_ref[...]
    vb = v.astype(BF16)

    bc = jnp.dot(tri_ref[...], logf, precision=HIGHEST, preferred_element_type=F32)
    blast = jnp.dot(tot_ref[...], logf, precision=HIGHEST, preferred_element_type=F32)
    qt = (q * jnp.exp(bc)).astype(BF16)
    kt = (kk * jnp.exp(-bc)).astype(BF16)
    kd = (kk * jnp.exp(blast - bc)).astype(BF16)
    chunk_decay = jnp.exp(blast)

    head = _head_of_lane((tc, width))
    head_sq = _head_of_lane((width, width))
    blockdiag = head_sq == lax.broadcasted_iota(jnp.int32, (width, width), 0) // HEAD_DIM
    rr = lax.broadcasted_iota(jnp.int32, (tc, tc), 0)
    cc = lax.broadcasted_iota(jnp.int32, (tc, tc), 1)
    keep = (cc <= rr) & (cc // chunk == rr // chunk)

    intra = jnp.zeros((tc, width), F32)
    for h in range(C_HEADS):
        att = lax.dot_general(jnp.where(head == h, qt, jnp.zeros_like(qt)), kt, (((1,), (1,)), ((), ())),
                              preferred_element_type=F32)
        att = jnp.where(keep, att, 0.0).astype(BF16)
        intra = jnp.where(head == h, jnp.dot(att, vb, preferred_element_type=F32), intra)

    for c in range(tc // chunk):
        r0 = c * chunk
        s_t = st_ref[...]
        inter = lax.dot_general(qt[r0:r0 + chunk], s_t.astype(BF16), (((1,), (1,)), ((), ())),
                                preferred_element_type=F32)
        o_ref[r0:r0 + chunk, :] = intra[r0:r0 + chunk] + inter
        upd = lax.dot_general(vb[r0:r0 + chunk], kd[r0:r0 + chunk], (((0,), (0,)), ((), ())),
                              preferred_element_type=F32)
        st_ref[...] = s_t * chunk_decay[r0:r0 + 1, :] + jnp.where(blockdiag, upd, 0.0)


def _hgrn(rest, log_lb, log_1m_lb, tri_bd, tot_bd, bsz, t):
    n = rest.shape[0]
    tc, nt = TC_HGRN, t // TC_HGRN
    blk = lambda col: pl.BlockSpec((tc, C_DIM), lambda b, j: (b * nt + j, col))
    const = lambda b, j: (0, 0)
    return pl.pallas_call(
        functools.partial(_hgrn_kernel, tc=tc, chunk=L_HGRN),
        grid=(bsz, nt),
        in_specs=[
            blk(4), blk(5), blk(6),
            pl.BlockSpec((1, C_DIM), const),
            pl.BlockSpec((1, C_DIM), const),
            pl.BlockSpec((tc, tc), const),
            pl.BlockSpec((tc, tc), const),
        ],
        out_specs=pl.BlockSpec((tc, C_DIM), lambda b, j: (b * nt + j, 0)),
        out_shape=jax.ShapeDtypeStruct((n, C_DIM), F32),
        scratch_shapes=[pltpu.VMEM((C_DIM, C_DIM), F32)],
        compiler_params=pltpu.CompilerParams(dimension_semantics=("arbitrary", "arbitrary"),
                                             vmem_limit_bytes=VMEM_LIMIT),
        name="hgrn2",
    )(rest, rest, rest, log_lb, log_1m_lb, tri_bd, tot_bd)


def _outproj_kernel(x_ref, oa_ref, ob_ref, oc_ref, bo_ref, cg_ref, g_ref, grp_ref, w_ref, o_ref):
    def normed(o, lo, hi):
        return o * lax.rsqrt(_group_mean_sq(o, grp_ref) + EPS) * g_ref[:, lo:hi]

    ya = normed(oa_ref[...], 0, A_DIM)
    yb = _sigmoid(bo_ref[...]) * normed(ob_ref[...], A_DIM, A_DIM + B_DIM)
    cg = cg_ref[...]
    yc = cg * _sigmoid(cg) * normed(oc_ref[...], A_DIM + B_DIM, D_MIX)
    acc = x_ref[...] + jnp.dot(ya.astype(BF16), w_ref[0:A_DIM, :], preferred_element_type=F32)
    acc = acc + jnp.dot(yb.astype(BF16), w_ref[A_DIM:A_DIM + B_DIM, :], preferred_element_type=F32)
    acc = acc + jnp.dot(yc.astype(BF16), w_ref[A_DIM + B_DIM:D_MIX, :], preferred_element_type=F32)
    o_ref[...] = acc


def _outproj(x2, oa, ob, oc, rest, out_g, grp, w_out):
    n = x2.shape[0]
    tm = TM_PROJ
    const = lambda i: (0, 0)
    return pl.pallas_call(
        _outproj_kernel,
        grid=(n // tm,),
        in_specs=[
            pl.BlockSpec((tm, D_MODEL), lambda i: (i, 0)),
            pl.BlockSpec((tm, A_DIM), lambda i: (i, 0)),
            pl.BlockSpec((tm, B_DIM), lambda i: (i, 0)),
            pl.BlockSpec((tm, C_DIM), lambda i: (i, 0)),
            pl.BlockSpec((tm, B_DIM), lambda i: (i, 3)),
            pl.BlockSpec((tm, C_DIM), lambda i: (i, 7)),
            pl.BlockSpec((1, D_MIX), const),
            pl.BlockSpec((A_DIM, A_DIM), const),
            pl.BlockSpec((D_MIX, D_MODEL), const),
        ],
        out_specs=pl.BlockSpec((tm, D_MODEL), lambda i: (i, 0)),
        out_shape=jax.ShapeDtypeStruct((n, D_MODEL), F32),
        compiler_params=pltpu.CompilerParams(dimension_semantics=("arbitrary",), vmem_limit_bytes=VMEM_LIMIT),
        name="outproj",
    )(x2, oa, ob, oc, rest, rest, out_g, grp, w_out)


def _ffn_kernel(x_ref, halo_ref, g_ref, wup_ref, cw_ref, cb_ref, wdn_ref, o_ref, h_ref, ug_ref, uu_ref,
                *, tm, tiles_per_seq):
    i = pl.program_id(0)
    halo_rows = BF16_ROWS

    def rms(x):
        ms = jnp.mean(x * x, axis=-1, keepdims=True)
        return x * lax.rsqrt(ms + EPS) * g_ref[...]

    x = x_ref[...]
    first = (i % tiles_per_seq) == 0
    h_ref[0:halo_rows, :] = jnp.where(first, 0.0, rms(halo_ref[...])).astype(BF16)
    h_ref[halo_rows:, :] = rms(x).astype(BF16)
    h = h_ref[...]

    def conv(u_ref, lo):
        y = cb_ref[:, lo:lo + FF_CHUNK] + cw_ref[FFN_CONV - 1:FFN_CONV, lo:lo + FF_CHUNK] * u_ref[halo_rows:, :]
        for d in range(1, FFN_CONV):
            y = y + cw_ref[FFN_CONV - 1 - d:FFN_CONV - d, lo:lo + FF_CHUNK] * u_ref[pl.ds(halo_rows - d, tm), :]
        return y

    acc = x
    for c in range(D_FF // FF_CHUNK):
        lo = c * FF_CHUNK
        ug_ref[...] = jnp.dot(h, wup_ref[:, lo:lo + FF_CHUNK], preferred_element_type=F32)
        uu_ref[...] = jnp.dot(h, wup_ref[:, D_FF + lo:D_FF + lo + FF_CHUNK], preferred_element_type=F32)
        gate = conv(ug_ref, lo)
        up = conv(uu_ref, D_FF + lo)
        act = (gate * _sigmoid(gate) * up).astype(BF16)
        acc = acc + jnp.dot(act, wdn_ref[lo:lo + FF_CHUNK, :], preferred_element_type=F32)
    o_ref[...] = acc


def _ffn(x2, g, w_up, conv_w, conv_b, w_down, t):
    n = x2.shape[0]
    tm = TM_PROJ
    halo = BF16_ROWS
    const = lambda i: (0, 0)
    return pl.pallas_call(
        functools.partial(_ffn_kernel, tm=tm, tiles_per_seq=t // tm),
        grid=(n // tm,),
        in_specs=[
            pl.BlockSpec((tm, D_MODEL), lambda i: (i, 0)),
            pl.BlockSpec((halo, D_MODEL), lambda i: (jnp.maximum(i * (tm // halo) - 1, 0), 0)),
            pl.BlockSpec((1, D_MODEL), const),
            pl.BlockSpec((D_MODEL, 2 * D_FF), const),
            pl.BlockSpec((FFN_CONV, 2 * D_FF), const),
            pl.BlockSpec((1, 2 * D_FF), const),
            pl.BlockSpec((D_FF, D_MODEL), const),
        ],
        out_specs=pl.BlockSpec((tm, D_MODEL), lambda i: (i, 0)),
        out_shape=jax.ShapeDtypeStruct((n, D_MODEL), F32),
        scratch_shapes=[
            pltpu.VMEM((tm + halo, D_MODEL), BF16),
            pltpu.VMEM((tm + halo, FF_CHUNK), F32),
            pltpu.VMEM((tm + halo, FF_CHUNK), F32),
        ],
        compiler_params=pltpu.CompilerParams(dimension_semantics=("arbitrary",), vmem_limit_bytes=VMEM_LIMIT),
        name="ffn",
    )(x2, x2, g, w_up, conv_w, conv_b, w_down)


def _block_diag_ones(n, blk, dtype):
    r = np.arange(n) // blk
    return jnp.asarray(r[:, None] == r[None, :], dtype)


def kernel(x, lb_logits, norm_mix_g, w_in, b_in, a_q_g, a_k_g, b_conv_w, out_g, w_out, norm_ffn_g, w_up,
           ffn_conv_w, ffn_conv_b, w_down):
    bsz, t, d = x.shape
    depth = w_in.shape[0]
    assert d == D_MODEL and t % TQ_FOX == 0 and t % TC_MLSTM == 0 and t % TC_HGRN == 0
    n = bsz * t
    x2 = x.reshape(n, d).astype(F32)

    p = jax.nn.softmax(lb_logits.astype(F32), axis=0)
    lb_all = jnp.maximum(jnp.cumsum(p, axis=0) - p[0], 0.0)
    log_lb = jnp.log(lb_all)
    log_1m_lb = jnp.log1p(-lb_all)

    grp_a = _block_diag_ones(A_DIM, HEAD_DIM, BF16)
    grp_b = _block_diag_ones(B_DIM, HEAD_DIM, F32)
    ar = np.arange(L_MLSTM)
    tri_u = jnp.asarray(ar[:, None] <= ar[None, :], F32)
    tri_l = jnp.asarray(ar[:, None] >= ar[None, :], F32)
    ah = np.arange(TC_HGRN)
    same = (ah[:, None] // L_HGRN) == (ah[None, :] // L_HGRN)
    tri_bd = jnp.asarray(same & (ah[:, None] >= ah[None, :]), F32)
    tot_bd = jnp.asarray(same, F32)

    for l in range(depth):
        w_r = _reorder_in_cols(w_in[l].astype(BF16))
        b_r = _reorder_in_cols(b_in[l].astype(F32))[None, :]
        gq = (jnp.tile(a_q_g[l].astype(F32), A_HEADS) * (HEAD_DIM ** -0.5 * LOG2E))[None, :]
        gk = jnp.tile(a_k_g[l].astype(F32), A_HEADS)[None, :]
        qkv, rest, gates = _inproj(x2, norm_mix_g[l][None, :].astype(F32), w_r, b_r, gq, gk, grp_a)

        g_t = gates[:, :16].reshape(bsz, t, 16).transpose(0, 2, 1)
        c_t, g_tm = _gates(g_t, tri_u)
        oa = _fox(qkv, c_t.reshape(bsz * (A_HEADS // 2), 2, t), bsz, t)
        ob = _mlstm(rest, g_tm, b_conv_w[l].astype(F32), tri_l, grp_b, bsz, t)
        oc = _hgrn(rest, log_lb[l][None, :], log_1m_lb[l][None, :], tri_bd, tot_bd, bsz, t)
        x2 = _outproj(x2, oa, ob, oc, rest, out_g[l][None, :].astype(F32), grp_a, w_out[l].astype(BF16))
        x2 = _ffn(x2, norm_ffn_g[l][None, :].astype(F32), w_up[l].astype(BF16), ffn_conv_w[l].astype(F32),
                  ffn_conv_b[l][None, :].astype(F32), w_down[l].astype(BF16), t)
    return x2.reshape(bsz, t, d).astype(x.dtype)
```

```python
import functools

import numpy as np
import jax
import jax.numpy as jnp
from jax import lax
from jax.experimental import pallas as pl
from jax.experimental.pallas import tpu as pltpu

F32 = jnp.float32
BF16 = jnp.bfloat16
HIGHEST = lax.Precision.HIGHEST
NEG_INF = float("-inf")
LOG2E = 1.4426950408889634

D_MODEL = 1024
HEAD_DIM = 64
A_HEADS, B_HEADS, C_HEADS = 8, 4, 4
A_DIM, B_DIM, C_DIM = A_HEADS * HEAD_DIM, B_HEADS * HEAD_DIM, C_HEADS * HEAD_DIM
D_MIX = A_DIM + B_DIM + C_DIM
D_FF = 2816
MLSTM_CONV = 4
FFN_CONV = 3
EPS = 1e-6

LANE = 128
BF16_ROWS = 16
VMEM_LIMIT = 56 * 1024 * 1024

QKV_W = 3 * A_DIM
REST_W = 10 * B_DIM
IN_W = QKV_W + REST_W + LANE

TM_PROJ = 512
TQ_FOX = 1024
FOX_BOUND_MAX = 40.0
TC_MLSTM = 512
L_MLSTM = 128
TC_HGRN = 256
L_HGRN = 16
FF_CHUNK = 256
FF_GROUP = 4


def _reorder_in_cols(w):
    a_f = 3 * A_DIM
    b_qk = a_f + A_HEADS
    b_v = b_qk + 2 * B_DIM
    b_i = b_v + B_DIM
    b_f = b_i + B_HEADS
    b_o = b_f + B_HEADS
    c_q = b_o + B_DIM
    end = c_q + 4 * C_DIM
    sl = lambda lo, hi: lax.slice_in_dim(w, lo, hi, axis=-1)
    n_gate = A_HEADS + 2 * B_HEADS
    pad = jnp.zeros(w.shape[:-1] + (LANE - n_gate,), w.dtype)
    out = jnp.concatenate([
        sl(0, a_f), sl(b_qk, b_i), sl(b_o, end),
        jnp.repeat(sl(b_i, b_f), HEAD_DIM, axis=-1), jnp.repeat(sl(b_f, b_o), HEAD_DIM, axis=-1),
        sl(a_f, b_qk), sl(b_i, b_o), pad], axis=-1)
    assert out.shape[-1] == IN_W
    return out


def _log_sigmoid(x):
    return jnp.minimum(x, 0.0) - jnp.log1p(jnp.exp(-jnp.abs(x)))


def _sigmoid(x):
    return 1.0 / (1.0 + jnp.exp(-x))


def _head_of_lane(shape):
    return lax.broadcasted_iota(jnp.int32, shape, len(shape) - 1) // HEAD_DIM


def _expand_heads(cols, head):
    out = cols[-1]
    for h in range(len(cols) - 2, -1, -1):
        out = jnp.where(head == h, cols[h], out)
    return out


def _group_mean_sq(z, grp_ref):
    w = z.shape[-1]
    zz = (z * z).astype(BF16)
    return jnp.dot(zz, grp_ref[0:w, 0:w], preferred_element_type=F32) * (1.0 / HEAD_DIM)


def _split3(x):
    hi = x.astype(BF16).astype(F32)
    mid = (x - hi).astype(BF16).astype(F32)
    lo = (x - hi - mid).astype(BF16).astype(F32)
    return hi, mid, lo


def _inproj_kernel(x_ref, g_ref, w_ref, b_ref, gq_ref, gk_ref, grp_ref, tri_ref, bound_ref,
                   qa_ref, ka_ref, va_ref, rest_ref, gates_ref, carry_ref, *, tiles_per_seq):
    i = pl.program_id(0)
    tm = x_ref.shape[0]

    @pl.when(i % tiles_per_seq == 0)
    def _():
        carry_ref[...] = jnp.zeros_like(carry_ref)

    x = x_ref[...]
    ms = jnp.mean(x * x, axis=-1, keepdims=True)
    h = (x * lax.rsqrt(ms + EPS) * g_ref[...]).astype(BF16)

    def proj(lo, hi):
        return jnp.dot(h, w_ref[:, lo:hi], preferred_element_type=F32) + b_ref[:, lo:hi]

    gates = proj(QKV_W + REST_W, IN_W)
    gates_ref[...] = gates
    q = proj(0, A_DIM)
    k = proj(A_DIM, 2 * A_DIM)
    v = proj(2 * A_DIM, QKV_W)

    tri = tri_ref[...]
    cs = None
    for part in _split3(_log_sigmoid(gates)):
        d = jnp.dot(tri, part.astype(BF16), preferred_element_type=F32)
        cs = d if cs is None else cs + d
    c2 = (cs + carry_ref[...]) * LOG2E
    carry_ref[...] = carry_ref[...] + cs[tm - 1:tm, :]

    q = q * lax.rsqrt(_group_mean_sq(q, grp_ref) + EPS) * gq_ref[...]
    k = k * lax.rsqrt(_group_mean_sq(k, grp_ref) + EPS) * gk_ref[...]
    for c in range(REST_W // A_DIM):
        rest_ref[:, c * A_DIM:(c + 1) * A_DIM] = proj(QKV_W + c * A_DIM, QKV_W + (c + 1) * A_DIM)

    lane = lax.broadcasted_iota(jnp.int32, (tm, LANE), 1)
    ones_q = jnp.where((lane >= HEAD_DIM) & (lane < HEAD_DIM + 3), 1.0, 0.0)
    ones_k = jnp.where((lane >= HEAD_DIM + 3) & (lane < HEAD_DIM + 6), 1.0, 0.0)
    ones_v = jnp.where(lane == HEAD_DIM, 1.0, 0.0)
    data = lane < HEAD_DIM
    for hd in range(A_HEADS):
        pair = slice((hd // 2) * LANE, (hd // 2 + 1) * LANE)
        out = slice(hd * LANE, (hd + 1) * LANE)
        place = (lambda z: z) if hd % 2 == 0 else (lambda z: pltpu.roll(z, HEAD_DIM, 1))
        cb = jnp.broadcast_to(c2[:, hd:hd + 1], (tm, LANE))
        qh, qm, ql = _split3(cb - bound_ref[...])
        kh, km, kl = _split3(cb)
        aug_q = jnp.where(lane == HEAD_DIM + 3, qh, jnp.where(lane == HEAD_DIM + 4, qm,
                          jnp.where(lane == HEAD_DIM + 5, ql, ones_q)))
        aug_k = jnp.where(lane == HEAD_DIM, -kh, jnp.where(lane == HEAD_DIM + 1, -km,
                          jnp.where(lane == HEAD_DIM + 2, -kl, ones_k)))
        qa_ref[:, out] = jnp.where(data, place(q[:, pair]), aug_q).astype(BF16)
        ka_ref[:, out] = jnp.where(data, place(k[:, pair]), aug_k).astype(BF16)
        va_ref[:, out] = jnp.where(data, place(v[:, pair]), ones_v).astype(BF16)


def _inproj(x2, g, w, b, gq, gk, grp, tri, bound, t):
    n = x2.shape[0]
    tm = TM_PROJ
    aug_w = A_HEADS * LANE
    const = lambda i: (0, 0)
    row = lambda width: pl.BlockSpec((tm, width), lambda i: (i, 0))
    return pl.pallas_call(
        functools.partial(_inproj_kernel, tiles_per_seq=t // tm),
        grid=(n // tm,),
        in_specs=[
            row(D_MODEL),
            pl.BlockSpec((1, D_MODEL), const),
            pl.BlockSpec((D_MODEL, IN_W), const),
            pl.BlockSpec((1, IN_W), const),
            pl.BlockSpec((1, A_DIM), const),
            pl.BlockSpec((1, A_DIM), const),
            pl.BlockSpec((A_DIM, A_DIM), const),
            pl.BlockSpec((tm, tm), const),
            pl.BlockSpec((1, LANE), const),
        ],
        out_specs=[row(aug_w), row(aug_w), row(aug_w), row(REST_W), row(LANE)],
        out_shape=[
            jax.ShapeDtypeStruct((n, aug_w), BF16),
            jax.ShapeDtypeStruct((n, aug_w), BF16),
            jax.ShapeDtypeStruct((n, aug_w), BF16),
            jax.ShapeDtypeStruct((n, REST_W), F32),
            jax.ShapeDtypeStruct((n, LANE), F32),
        ],
        scratch_shapes=[pltpu.VMEM((1, LANE), F32)],
        compiler_params=pltpu.CompilerParams(dimension_semantics=("arbitrary",), vmem_limit_bytes=VMEM_LIMIT),
        name="inproj",
    )(x2, g, w, b, gq, gk, grp, tri, bound)


def _fox_kernel(q_ref, k_ref, v_ref, o_ref, *, tq):
    i = pl.program_id(2)
    half = tq // 2
    nt = (((1,), (1,)), ((), ()))

    def attend(q, keys, hh, visible=None):
        cols = slice(hh * LANE, (hh + 1) * LANE)
        s = lax.dot_general(q, k_ref[keys, cols], nt, preferred_element_type=F32)
        if visible is not None:
            s = jnp.where(visible, s, NEG_INF)
        return jnp.dot(jnp.exp2(s).astype(BF16), v_ref[keys, cols], preferred_element_type=F32)

    def full_step(j, accs):
        keys = pl.ds(pl.multiple_of(j * tq, tq), tq)
        return tuple(accs[hh] + attend(q_ref[:, hh * LANE:(hh + 1) * LANE], keys, hh) for hh in range(2))

    zero = jnp.zeros((tq, LANE), F32)
    accs = lax.fori_loop(0, i, full_step, (zero, zero))

    off = pl.multiple_of(i * tq, tq)
    row_t = lax.broadcasted_iota(jnp.int32, (half, half), 0)
    col_t = lax.broadcasted_iota(jnp.int32, (half, half), 1)
    row_b = lax.broadcasted_iota(jnp.int32, (half, tq), 0) + half
    col_b = lax.broadcasted_iota(jnp.int32, (half, tq), 1)
    lane = lax.broadcasted_iota(jnp.int32, (half, LANE), 1)
    outs = []
    for hh in range(2):
        cols = slice(hh * LANE, (hh + 1) * LANE)
        top = accs[hh][:half] + attend(q_ref[:half, cols], pl.ds(off, half), hh, col_t <= row_t)
        bot = accs[hh][half:] + attend(q_ref[half:, cols], pl.ds(off, tq), hh, col_b <= row_b)
        outs.append((top / top[:, HEAD_DIM:HEAD_DIM + 1], bot / bot[:, HEAD_DIM:HEAD_DIM + 1]))
    for r, rows in enumerate((slice(0, half), slice(half, tq))):
        o_ref[rows, :] = jnp.where(lane < HEAD_DIM, outs[0][r], pltpu.roll(outs[1][r], HEAD_DIM, 1))


def _fox(qa, ka, va, bsz, t):
    n = qa.shape[0]
    tq = TQ_FOX
    nq = t // tq
    pairs = A_HEADS // 2
    return pl.pallas_call(
        functools.partial(_fox_kernel, tq=tq),
        grid=(bsz, pairs, nq),
        in_specs=[
            pl.BlockSpec((tq, 2 * LANE), lambda b, p, i: (b * nq + i, p)),
            pl.BlockSpec((t, 2 * LANE), lambda b, p, i: (b, p)),
            pl.BlockSpec((t, 2 * LANE), lambda b, p, i: (b, p)),
        ],
        out_specs=pl.BlockSpec((tq, LANE), lambda b, p, i: (b * nq + i, p)),
        out_shape=jax.ShapeDtypeStruct((n, A_DIM), F32),
        compiler_params=pltpu.CompilerParams(
            dimension_semantics=("arbitrary", "arbitrary", "arbitrary"), vmem_limit_bytes=VMEM_LIMIT),
        name="fox",
    )(qa, ka, va)


def _fox_safe_kernel(q_ref, k_ref, v_ref, c_ref, o_ref, *, tq):
    i = pl.program_id(2)
    q = q_ref[...]
    lane = lax.broadcasted_iota(jnp.int32, (tq, LANE), 1)
    zero = jnp.zeros_like(q)
    qh = (jnp.where(lane < HEAD_DIM, q, zero), jnp.where(lane >= HEAD_DIM, q, zero))
    tri = lax.broadcasted_iota(jnp.int32, (tq, tq), 1) <= lax.broadcasted_iota(jnp.int32, (tq, tq), 0)

    def step(j, carry, masked):
        off = pl.multiple_of(j * tq, tq)
        kt = k_ref[pl.ds(off, tq), :]
        vt = v_ref[pl.ds(off, tq), :]
        new = []
        for hh in range(2):
            m, l, acc = carry[hh]
            s = lax.dot_general(qh[hh], kt, (((1,), (1,)), ((), ())), preferred_element_type=F32)
            s = s - c_ref[0, hh:hh + 1, pl.ds(off, tq)]
            if masked:
                s = jnp.where(tri, s, NEG_INF)
            m_new = jnp.maximum(m, jnp.max(s, axis=-1, keepdims=True))
            alpha = jnp.exp2(m - m_new)
            p = jnp.exp2(s - m_new)
            l = alpha * l + jnp.sum(p, axis=-1, keepdims=True)
            acc = alpha * acc + jnp.dot(p.astype(BF16), vt, preferred_element_type=F32)
            new.append((m_new, l, acc))
        return tuple(new)

    init = tuple((jnp.full((tq, 1), NEG_INF, F32), jnp.zeros((tq, 1), F32), jnp.zeros((tq, LANE), F32))
                 for _ in range(2))
    carry = lax.fori_loop(0, i, lambda j, c: step(j, c, False), init)
    (_, l0, a0), (_, l1, a1) = step(i, carry, True)
    o_ref[...] = jnp.where(lane < HEAD_DIM, a0 / l0, a1 / l1)


def _fox_safe(qa, ka, va, bsz, t):
    n = qa.shape[0]
    tq = TQ_FOX // 2
    nq = t // tq
    pairs = A_HEADS // 2
    heads = lambda z: z.reshape(n, A_HEADS, LANE)
    unpack = lambda z: heads(z)[:, :, :HEAD_DIM].reshape(n, A_DIM)
    c = -jnp.sum(heads(ka)[:, :, HEAD_DIM:HEAD_DIM + 3].astype(F32), axis=-1)
    c_t = c.reshape(bsz, t, A_HEADS).transpose(0, 2, 1).reshape(bsz * pairs, 2, t)
    return pl.pallas_call(
        functools.partial(_fox_safe_kernel, tq=tq),
        grid=(bsz, pairs, nq),
        in_specs=[
            pl.BlockSpec((tq, LANE), lambda b, p, i: (b * nq + i, p)),
            pl.BlockSpec((t, LANE), lambda b, p, i: (b, p)),
            pl.BlockSpec((t, LANE), lambda b, p, i: (b, p)),
            pl.BlockSpec((1, 2, t), lambda b, p, i: (b * pairs + p, 0, 0)),
        ],
        out_specs=pl.BlockSpec((tq, LANE), lambda b, p, i: (b * nq + i, p)),
        out_shape=jax.ShapeDtypeStruct((n, A_DIM), F32),
        compiler_params=pltpu.CompilerParams(
            dimension_semantics=("arbitrary", "arbitrary", "arbitrary"), vmem_limit_bytes=VMEM_LIMIT),
        name="fox_safe",
    )(unpack(qa), unpack(ka), unpack(va), c_t)


def _mlstm_kernel(qk_ref, halo_ref, v_ref, bi_ref, bf_ref, gates_ref, cw_ref, tri_ref, triu_ref, grp_ref, o_ref,
                  buf_ref, c_st, n_st, m_st, *, tc, chunk):
    j = pl.program_id(1)
    width = B_DIM

    @pl.when(j == 0)
    def _():
        c_st[...] = jnp.zeros_like(c_st)
        n_st[...] = jnp.zeros_like(n_st)
        m_st[...] = jnp.full(m_st.shape, NEG_INF, F32)

    buf_ref[0:8, :] = jnp.where(j == 0, 0.0, halo_ref[...])
    buf_ref[8:8 + tc, :] = qk_ref[...]
    cw = cw_ref[...]
    y = cw[MLSTM_CONV - 1:MLSTM_CONV] * buf_ref[8:8 + tc, :]
    for d in range(1, MLSTM_CONV):
        y = y + cw[MLSTM_CONV - 1 - d:MLSTM_CONV - d] * buf_ref[8 - d:8 - d + tc, :]
    y = y * _sigmoid(y)
    q_all = y[:, :width]
    k_all = y[:, width:] * (HEAD_DIM ** -0.5)

    head = _head_of_lane((chunk, width))
    head_sq = _head_of_lane((width, width))
    blockdiag = head_sq == lax.broadcasted_iota(jnp.int32, (width, width), 0) // HEAD_DIM
    causal = lax.broadcasted_iota(jnp.int32, (chunk, chunk), 1) <= lax.broadcasted_iota(jnp.int32, (chunk, chunk), 0)
    tri = tri_ref[...]
    grp = grp_ref[...]

    for c in range(tc // chunk):
        r0 = c * chunk
        q = q_all[r0:r0 + chunk]
        k = k_all[r0:r0 + chunk]
        v = v_ref[r0:r0 + chunk, :]
        qb, kb, vb = q.astype(BF16), k.astype(BF16), v.astype(BF16)
        i_x = bi_ref[r0:r0 + chunk, :]
        bc_x = jnp.dot(tri, _log_sigmoid(bf_ref[r0:r0 + chunk, :]), precision=HIGHEST,
                       preferred_element_type=F32)
        c_prev, n_prev, m_prev = c_st[...], n_st[...], m_st[...]
        g_t = gates_ref[r0:r0 + chunk, :].T[A_HEADS:A_HEADS + 2 * B_HEADS, :]
        bc_t = jnp.dot(_log_sigmoid(g_t), triu_ref[...], precision=HIGHEST, preferred_element_type=F32)

        m_cols, den_cols = [], []
        num_x = jnp.zeros((chunk, width), F32)
        for h in range(B_HEADS):
            lo = h * HEAD_DIM
            bc_col = bc_x[:, lo:lo + 1]
            row = g_t[h:h + 1, :] - bc_t[B_HEADS + h:B_HEADS + h + 1, :]
            log_d = jnp.where(causal, bc_col + row, NEG_INF)
            m_out = jnp.maximum(bc_col + m_prev[:, lo:lo + 1], jnp.max(log_d, axis=-1, keepdims=True))
            s = lax.dot_general(jnp.where(head == h, qb, jnp.zeros_like(qb)), kb, (((1,), (1,)), ((), ())),
                                preferred_element_type=F32)
            sqk = s * jnp.exp(log_d - m_out)
            num_x = jnp.where(head == h, jnp.dot(sqk.astype(BF16), vb, preferred_element_type=F32), num_x)
            m_cols.append(m_out)
            den_cols.append(jnp.sum(sqk, axis=-1, keepdims=True))
        m_out_x = _expand_heads(m_cols, head)
        den_x = _expand_heads(den_cols, head)

        inter_w = jnp.exp(bc_x + m_prev - m_out_x)
        q_c = jnp.dot(qb, c_prev.astype(BF16), preferred_element_type=F32)
        q_n = jnp.dot(q * n_prev, grp, precision=HIGHEST, preferred_element_type=F32)
        num = num_x + inter_w * q_c
        den = den_x + inter_w * q_n
        o_ref[r0:r0 + chunk, :] = num / jnp.maximum(jnp.abs(den), jnp.exp(-m_out_x))

        g_x = bc_x[chunk - 1:chunk, :]
        a_x = g_x - bc_x + i_x
        m_new = jnp.maximum(g_x + m_prev, jnp.max(a_x, axis=0, keepdims=True))
        decay = jnp.exp(g_x + m_prev - m_new)
        kw = k * jnp.exp(a_x - m_new)
        upd = lax.dot_general(kw.astype(BF16), vb, (((0,), (0,)), ((), ())), preferred_element_type=F32)
        c_st[...] = decay * c_prev + jnp.where(blockdiag, upd, 0.0)
        n_st[...] = decay * n_prev + jnp.sum(kw, axis=0, keepdims=True)
        m_st[...] = m_new


def _mlstm(rest, gates, conv_w, tri_l, tri_u, grp, bsz, t):
    n = rest.shape[0]
    tc, nt = TC_MLSTM, t // TC_MLSTM
    blk = lambda col: pl.BlockSpec((tc, B_DIM), lambda b, j: (b * nt + j, col))
    const = lambda b, j: (0, 0)
    return pl.pallas_call(
        functools.partial(_mlstm_kernel, tc=tc, chunk=L_MLSTM),
        grid=(bsz, nt),
        in_specs=[
            pl.BlockSpec((tc, 2 * B_DIM), lambda b, j: (b * nt + j, 0)),
            pl.BlockSpec((8, 2 * B_DIM), lambda b, j: (jnp.maximum((b * nt + j) * (tc // 8) - 1, 0), 0)),
            blk(2), blk(8), blk(9),
            pl.BlockSpec((tc, LANE), lambda b, j: (b * nt + j, 0)),
            pl.BlockSpec((MLSTM_CONV, 2 * B_DIM), const),
            pl.BlockSpec((L_MLSTM, L_MLSTM), const),
            pl.BlockSpec((L_MLSTM, L_MLSTM), const),
            pl.BlockSpec((B_DIM, B_DIM), const),
        ],
        out_specs=pl.BlockSpec((tc, B_DIM), lambda b, j: (b * nt + j, 0)),
        out_shape=jax.ShapeDtypeStruct((n, B_DIM), F32),
        scratch_shapes=[
            pltpu.VMEM((tc + 8, 2 * B_DIM), F32),
            pltpu.VMEM((B_DIM, B_DIM), F32),
            pltpu.VMEM((1, B_DIM), F32),
            pltpu.VMEM((1, B_DIM), F32),
        ],
        compiler_params=pltpu.CompilerParams(dimension_semantics=("arbitrary", "arbitrary"),
                                             vmem_limit_bytes=VMEM_LIMIT),
        name="mlstm",
    )(rest, rest, rest, rest, rest, gates, conv_w, tri_l, tri_u, grp)


def _hgrn_kernel(q_ref, f_ref, i_ref, la_ref, l1m_ref, tri_ref, tot_ref, o_ref, st_ref, *, tc, chunk):
    j = pl.program_id(1)
    width = C_DIM

    @pl.when(j == 0)
    def _():
        st_ref[...] = jnp.zeros_like(st_ref)

    xq = q_ref[...]
    q = xq * _sigmoid(xq)
    xf = f_ref[...]
    ls = _log_sigmoid(xf)
    a = la_ref[...]
    b = l1m_ref[...] + ls
    logf = jnp.maximum(a, b) + jnp.log1p(jnp.exp(-jnp.abs(a - b)))
    kk = jnp.exp(l1m_ref[...] + ls - xf)
    v = i_ref[...]
    vb = v.astype(BF16)

    bc = jnp.dot(tri_ref[...], logf, precision=HIGHEST, preferred_element_type=F32)
    blast = jnp.dot(tot_ref[...], logf, precision=HIGHEST, preferred_element_type=F32)
    qt = (q * jnp.exp(bc)).astype(BF16)
    kt = (kk * jnp.exp(-bc)).astype(BF16)
    kd = (kk * jnp.exp(blast - bc)).astype(BF16)
    chunk_decay = jnp.exp(blast)

    head = _head_of_lane((tc, width))
    head_sq = _head_of_lane((width, width))
    blockdiag = head_sq == lax.broadcasted_iota(jnp.int32, (width, width), 0) // HEAD_DIM
    rr = lax.broadcasted_iota(jnp.int32, (tc, tc), 0)
    cc = lax.broadcasted_iota(jnp.int32, (tc, tc), 1)
    keep = (cc <= rr) & (cc // chunk == rr // chunk)

    intra = jnp.zeros((tc, width), F32)
    for h in range(C_HEADS):
        att = lax.dot_general(jnp.where(head == h, qt, jnp.zeros_like(qt)), kt, (((1,), (1,)), ((), ())),
                              preferred_element_type=F32)
        att = jnp.where(keep, att, 0.0).astype(BF16)
        intra = jnp.where(head == h, jnp.dot(att, vb, preferred_element_type=F32), intra)

    for c in range(tc // chunk):
        r0 = c * chunk
        s_t = st_ref[...]
        inter = lax.dot_general(qt[r0:r0 + chunk], s_t.astype(BF16), (((1,), (1,)), ((), ())),
                                preferred_element_type=F32)
        o_ref[r0:r0 + chunk, :] = intra[r0:r0 + chunk] + inter
        upd = lax.dot_general(vb[r0:r0 + chunk], kd[r0:r0 + chunk], (((0,), (0,)), ((), ())),
                              preferred_element_type=F32)
        st_ref[...] = s_t * chunk_decay[r0:r0 + 1, :] + jnp.where(blockdiag, upd, 0.0)


def _hgrn(rest, log_lb, log_1m_lb, tri_bd, tot_bd, bsz, t):
    n = rest.shape[0]
    tc, nt = TC_HGRN, t // TC_HGRN
    blk = lambda col: pl.BlockSpec((tc, C_DIM), lambda b, j: (b * nt + j, col))
    const = lambda b, j: (0, 0)
    return pl.pallas_call(
        functools.partial(_hgrn_kernel, tc=tc, chunk=L_HGRN),
        grid=(bsz, nt),
        in_specs=[
            blk(4), blk(5), blk(6),
            pl.BlockSpec((1, C_DIM), const),
            pl.BlockSpec((1, C_DIM), const),
            pl.BlockSpec((tc, tc), const),
            pl.BlockSpec((tc, tc), const),
        ],
        out_specs=pl.BlockSpec((tc, C_DIM), lambda b, j: (b * nt + j, 0)),
        out_shape=jax.ShapeDtypeStruct((n, C_DIM), F32),
        scratch_shapes=[pltpu.VMEM((C_DIM, C_DIM), F32)],
        compiler_params=pltpu.CompilerParams(dimension_semantics=("arbitrary", "arbitrary"),
                                             vmem_limit_bytes=VMEM_LIMIT),
        name="hgrn2",
    )(rest, rest, rest, log_lb, log_1m_lb, tri_bd, tot_bd)


def _outproj_kernel(x_ref, oa_ref, ob_ref, oc_ref, bo_ref, cg_ref, g_ref, grp_ref, w_ref, o_ref):
    def normed(o, lo, hi):
        return o * lax.rsqrt(_group_mean_sq(o, grp_ref) + EPS) * g_ref[:, lo:hi]

    ya = normed(oa_ref[...], 0, A_DIM)
    yb = _sigmoid(bo_ref[...]) * normed(ob_ref[...], A_DIM, A_DIM + B_DIM)
    cg = cg_ref[...]
    yc = cg * _sigmoid(cg) * normed(oc_ref[...], A_DIM + B_DIM, D_MIX)
    acc = x_ref[...] + jnp.dot(ya.astype(BF16), w_ref[0:A_DIM, :], preferred_element_type=F32)
    acc = acc + jnp.dot(yb.astype(BF16), w_ref[A_DIM:A_DIM + B_DIM, :], preferred_element_type=F32)
    acc = acc + jnp.dot(yc.astype(BF16), w_ref[A_DIM + B_DIM:D_MIX, :], preferred_element_type=F32)
    o_ref[...] = acc


def _outproj(x2, oa, ob, oc, rest, out_g, grp, w_out):
    n = x2.shape[0]
    tm = TM_PROJ
    const = lambda i: (0, 0)
    return pl.pallas_call(
        _outproj_kernel,
        grid=(n // tm,),
        in_specs=[
            pl.BlockSpec((tm, D_MODEL), lambda i: (i, 0)),
            pl.BlockSpec((tm, A_DIM), lambda i: (i, 0)),
            pl.BlockSpec((tm, B_DIM), lambda i: (i, 0)),
            pl.BlockSpec((tm, C_DIM), lambda i: (i, 0)),
            pl.BlockSpec((tm, B_DIM), lambda i: (i, 3)),
            pl.BlockSpec((tm, C_DIM), lambda i: (i, 7)),
            pl.BlockSpec((1, D_MIX), const),
            pl.BlockSpec((A_DIM, A_DIM), const),
            pl.BlockSpec((D_MIX, D_MODEL), const),
        ],
        out_specs=pl.BlockSpec((tm, D_MODEL), lambda i: (i, 0)),
        out_shape=jax.ShapeDtypeStruct((n, D_MODEL), F32),
        compiler_params=pltpu.CompilerParams(dimension_semantics=("arbitrary",), vmem_limit_bytes=VMEM_LIMIT),
        name="outproj",
    )(x2, oa, ob, oc, rest, rest, out_g, grp, w_out)


def _ffn_kernel(x_ref, halo_ref, g_ref, wup_ref, cw_ref, cb_ref, wdn_ref, o_ref, h_ref, ug_ref, uu_ref, act_ref,
                *, tm, tiles_per_seq):
    i = pl.program_id(0)
    halo_rows = BF16_ROWS

    def rms(x):
        ms = jnp.mean(x * x, axis=-1, keepdims=True)
        return x * lax.rsqrt(ms + EPS) * g_ref[...]

    x = x_ref[...]
    first = (i % tiles_per_seq) == 0
    h_ref[0:halo_rows, :] = jnp.where(first, 0.0, rms(halo_ref[...])).astype(BF16)
    h_ref[halo_rows:, :] = rms(x).astype(BF16)
    h = h_ref[...]

    def conv(u_ref, slot, lo):
        y = cb_ref[:, lo:lo + FF_CHUNK] + cw_ref[FFN_CONV - 1:FFN_CONV, lo:lo + FF_CHUNK] * u_ref[slot, halo_rows:, :]
        for d in range(1, FFN_CONV):
            y = y + cw_ref[FFN_CONV - 1 - d:FFN_CONV - d, lo:lo + FF_CHUNK] * u_ref[slot, pl.ds(halo_rows - d, tm), :]
        return y

    def up_proj(c):
        lo = c * FF_CHUNK
        ug_ref[c % 2] = jnp.dot(h, wup_ref[:, lo:lo + FF_CHUNK], preferred_element_type=F32)
        uu_ref[c % 2] = jnp.dot(h, wup_ref[:, D_FF + lo:D_FF + lo + FF_CHUNK], preferred_element_type=F32)

    n_chunks = D_FF // FF_CHUNK
    acc = x
    up_proj(0)
    for c in range(n_chunks):
        if c + 1 < n_chunks:
            up_proj(c + 1)
        lo = c * FF_CHUNK
        gate = conv(ug_ref, c % 2, lo)
        up = conv(uu_ref, c % 2, D_FF + lo)
        act_ref[:, lo:lo + FF_CHUNK] = (gate * _sigmoid(gate) * up).astype(BF16)
        if (c + 1) % FF_GROUP == 0 or c + 1 == n_chunks:
            g0 = (c // FF_GROUP) * FF_GROUP * FF_CHUNK
            acc = acc + jnp.dot(act_ref[:, g0:lo + FF_CHUNK], wdn_ref[g0:lo + FF_CHUNK, :],
                                preferred_element_type=F32)
    o_ref[...] = acc


def _ffn(x2, g, w_up, conv_w, conv_b, w_down, t):
    n = x2.shape[0]
    tm = TM_PROJ
    halo = BF16_ROWS
    const = lambda i: (0, 0)
    return pl.pallas_call(
        functools.partial(_ffn_kernel, tm=tm, tiles_per_seq=t // tm),
        grid=(n // tm,),
        in_specs=[
            pl.BlockSpec((tm, D_MODEL), lambda i: (i, 0)),
            pl.BlockSpec((halo, D_MODEL), lambda i: (jnp.maximum(i * (tm // halo) - 1, 0), 0)),
            pl.BlockSpec((1, D_MODEL), const),
            pl.BlockSpec((D_MODEL, 2 * D_FF), const),
            pl.BlockSpec((FFN_CONV, 2 * D_FF), const),
            pl.BlockSpec((1, 2 * D_FF), const),
            pl.BlockSpec((D_FF, D_MODEL), const),
        ],
        out_specs=pl.BlockSpec((tm, D_MODEL), lambda i: (i, 0)),
        out_shape=jax.ShapeDtypeStruct((n, D_MODEL), F32),
        scratch_shapes=[
            pltpu.VMEM((tm + halo, D_MODEL), BF16),
            pltpu.VMEM((2, tm + halo, FF_CHUNK), F32),
            pltpu.VMEM((2, tm + halo, FF_CHUNK), F32),
            pltpu.VMEM((tm, D_FF), BF16),
        ],
        compiler_params=pltpu.CompilerParams(dimension_semantics=("arbitrary",), vmem_limit_bytes=VMEM_LIMIT),
        name="ffn",
    )(x2, x2, g, w_up, conv_w, conv_b, w_down)


def _block_diag_ones(n, blk, dtype):
    r = np.arange(n) // blk
    return jnp.asarray(r[:, None] == r[None, :], dtype)


def kernel(x, lb_logits, norm_mix_g, w_in, b_in, a_q_g, a_k_g, b_conv_w, out_g, w_out, norm_ffn_g, w_up,
           ffn_conv_w, ffn_conv_b, w_down):
    bsz, t, d = x.shape
    depth = w_in.shape[0]
    assert d == D_MODEL and t % TQ_FOX == 0 and t % TC_MLSTM == 0 and t % TC_HGRN == 0
    n = bsz * t
    x2 = x.reshape(n, d).astype(F32)

    p = jax.nn.softmax(lb_logits.astype(F32), axis=0)
    lb_all = jnp.maximum(jnp.cumsum(p, axis=0) - p[0], 0.0)
    log_lb = jnp.log(lb_all)
    log_1m_lb = jnp.log1p(-lb_all)

    grp_a = _block_diag_ones(A_DIM, HEAD_DIM, BF16)
    grp_b = _block_diag_ones(B_DIM, HEAD_DIM, F32)
    ar = np.arange(L_MLSTM)
    tri_u = jnp.asarray(ar[:, None] <= ar[None, :], F32)
    tri_l = jnp.asarray(ar[:, None] >= ar[None, :], F32)
    ap = np.arange(TM_PROJ)
    tri_proj = jnp.asarray(ap[:, None] >= ap[None, :], BF16)
    ah = np.arange(TC_HGRN)
    same = (ah[:, None] // L_HGRN) == (ah[None, :] // L_HGRN)
    tri_bd = jnp.asarray(same & (ah[:, None] >= ah[None, :]), F32)
    tot_bd = jnp.asarray(same, F32)

    for l in range(depth):
        w_r = _reorder_in_cols(w_in[l].astype(BF16))
        b_r = _reorder_in_cols(b_in[l].astype(F32))[None, :]
        gq = (jnp.tile(a_q_g[l].astype(F32), A_HEADS) * (HEAD_DIM ** -0.5 * LOG2E))[None, :]
        gk = jnp.tile(a_k_g[l].astype(F32), A_HEADS)[None, :]
        bound = HEAD_DIM * jnp.max(jnp.abs(gq)) * jnp.max(jnp.abs(gk))
        qa, ka, va, rest, gates = _inproj(x2, norm_mix_g[l][None, :].astype(F32), w_r, b_r, gq, gk, grp_a, tri_proj,
                                          jnp.full((1, LANE), bound, F32), t)
        oa = lax.cond(bound < FOX_BOUND_MAX, functools.partial(_fox, bsz=bsz, t=t),
                      functools.partial(_fox_safe, bsz=bsz, t=t), qa, ka, va)
        ob = _mlstm(rest, gates, b_conv_w[l].astype(F32), tri_l, tri_u, grp_b, bsz, t)
        oc = _hgrn(rest, log_lb[l][None, :], log_1m_lb[l][None, :], tri_bd, tot_bd, bsz, t)
        x2 = _outproj(x2, oa, ob, oc, rest, out_g[l][None, :].astype(F32), grp_a, w_out[l].astype(BF16))
        x2 = _ffn(x2, norm_ffn_g[l][None, :].astype(F32), w_up[l].astype(BF16), ffn_conv_w[l].astype(F32),
                  ffn_conv_b[l][None, :].astype(F32), w_down[l].astype(BF16), t)
    return x2.reshape(bsz, t, d).astype(x.dtype)
```

```python
import functools

import numpy as np
import jax
import jax.numpy as jnp
from jax import lax
from jax.experimental import pallas as pl
from jax.experimental.pallas import tpu as pltpu

F32 = jnp.float32
BF16 = jnp.bfloat16
HIGHEST = lax.Precision.HIGHEST
NEG_INF = float("-inf")
LOG2E = 1.4426950408889634

D_MODEL = 1024
HEAD_DIM = 64
A_HEADS, B_HEADS, C_HEADS = 8, 4, 4
A_DIM, B_DIM, C_DIM = A_HEADS * HEAD_DIM, B_HEADS * HEAD_DIM, C_HEADS * HEAD_DIM
D_MIX = A_DIM + B_DIM + C_DIM
D_FF = 2816
MLSTM_CONV = 4
FFN_CONV = 3
EPS = 1e-6

LANE = 128
BF16_ROWS = 16
VMEM_LIMIT = 56 * 1024 * 1024

QKV_W = 3 * A_DIM
REST_W = 8 * B_DIM
IN_W = QKV_W + REST_W + LANE

TM_PROJ = 512
TQ_FOX = 1024
FOX_BOUND_MAX = 40.0
FOX_SKIP = 160.0
TC_MLSTM = 512
L_MLSTM = 128
TC_HGRN = 256
L_HGRN = 16
FF_CHUNK = 256


def _reorder_in_cols(w):
    a_f = 3 * A_DIM
    b_qk = a_f + A_HEADS
    b_v = b_qk + 2 * B_DIM
    b_i = b_v + B_DIM
    b_f = b_i + B_HEADS
    b_o = b_f + B_HEADS
    c_q = b_o + B_DIM
    end = c_q + 4 * C_DIM
    sl = lambda lo, hi: lax.slice_in_dim(w, lo, hi, axis=-1)
    n_gate = A_HEADS + 2 * B_HEADS
    pad = jnp.zeros(w.shape[:-1] + (LANE - n_gate,), w.dtype)
    out = jnp.concatenate([sl(0, a_f), sl(b_qk, b_i), sl(b_o, end), sl(a_f, b_qk), sl(b_i, b_o), pad], axis=-1)
    assert out.shape[-1] == IN_W
    return out


def _log_sigmoid(x):
    return jnp.minimum(x, 0.0) - jnp.log1p(jnp.exp(-jnp.abs(x)))


def _sigmoid(x):
    return 1.0 / (1.0 + jnp.exp(-x))


def _head_of_lane(shape):
    return lax.broadcasted_iota(jnp.int32, shape, len(shape) - 1) // HEAD_DIM


def _expand_heads(cols, head):
    out = cols[-1]
    for h in range(len(cols) - 2, -1, -1):
        out = jnp.where(head == h, cols[h], out)
    return out


def _group_mean_sq(z, grp_ref):
    w = z.shape[-1]
    zz = (z * z).astype(BF16)
    return jnp.dot(zz, grp_ref[0:w, 0:w], preferred_element_type=F32) * (1.0 / HEAD_DIM)


def _split3(x):
    hi = x.astype(BF16).astype(F32)
    mid = (x - hi).astype(BF16).astype(F32)
    lo = (x - hi - mid).astype(BF16).astype(F32)
    return hi, mid, lo


def _inproj_kernel(x_ref, g_ref, w_ref, b_ref, gq_ref, gk_ref, grp_ref, tri_ref, bound_ref,
                   qa_ref, ka_ref, va_ref, rest_ref, gates_ref, carry_ref, *, tiles_per_seq):
    i = pl.program_id(0)
    tm = x_ref.shape[0]

    @pl.when(i % tiles_per_seq == 0)
    def _():
        carry_ref[...] = jnp.zeros_like(carry_ref)

    x = x_ref[...]
    ms = jnp.mean(x * x, axis=-1, keepdims=True)
    h = (x * lax.rsqrt(ms + EPS) * g_ref[...]).astype(BF16)

    def proj(lo, hi):
        return jnp.dot(h, w_ref[:, lo:hi], preferred_element_type=F32) + b_ref[:, lo:hi]

    gates = proj(QKV_W + REST_W, IN_W)
    gates_ref[...] = gates
    q = proj(0, A_DIM)
    k = proj(A_DIM, 2 * A_DIM)
    v = proj(2 * A_DIM, QKV_W)

    tri = tri_ref[...]
    cs = None
    for part in _split3(_log_sigmoid(gates)):
        d = jnp.dot(tri, part.astype(BF16), preferred_element_type=F32)
        cs = d if cs is None else cs + d
    c2 = (cs + carry_ref[...]) * LOG2E
    carry_ref[...] = carry_ref[...] + cs[tm - 1:tm, :]

    q = q * lax.rsqrt(_group_mean_sq(q, grp_ref) + EPS) * gq_ref[...]
    k = k * lax.rsqrt(_group_mean_sq(k, grp_ref) + EPS) * gk_ref[...]
    for c in range(REST_W // A_DIM):
        rest_ref[:, c * A_DIM:(c + 1) * A_DIM] = proj(QKV_W + c * A_DIM, QKV_W + (c + 1) * A_DIM).astype(BF16)

    lane = lax.broadcasted_iota(jnp.int32, (tm, LANE), 1)
    ones_q = jnp.where((lane >= HEAD_DIM) & (lane < HEAD_DIM + 3), 1.0, 0.0)
    ones_k = jnp.where((lane >= HEAD_DIM + 3) & (lane < HEAD_DIM + 6), 1.0, 0.0)
    ones_v = jnp.where(lane == HEAD_DIM, 1.0, 0.0)
    data = lane < HEAD_DIM
    for hd in range(A_HEADS):
        pair = slice((hd // 2) * LANE, (hd // 2 + 1) * LANE)
        out = slice(hd * LANE, (hd + 1) * LANE)
        place = (lambda z: z) if hd % 2 == 0 else (lambda z: pltpu.roll(z, HEAD_DIM, 1))
        cb = jnp.broadcast_to(c2[:, hd:hd + 1], (tm, LANE))
        qh, qm, ql = _split3(cb - bound_ref[...])
        kh, km, kl = _split3(cb)
        aug_q = jnp.where(lane == HEAD_DIM + 3, qh, jnp.where(lane == HEAD_DIM + 4, qm,
                          jnp.where(lane == HEAD_DIM + 5, ql, ones_q)))
        aug_k = jnp.where(lane == HEAD_DIM, -kh, jnp.where(lane == HEAD_DIM + 1, -km,
                          jnp.where(lane == HEAD_DIM + 2, -kl, ones_k)))
        qa_ref[:, out] = jnp.where(data, place(q[:, pair]), aug_q).astype(BF16)
        ka_ref[:, out] = jnp.where(data, place(k[:, pair]), aug_k).astype(BF16)
        va_ref[:, out] = jnp.where(data, place(v[:, pair]), ones_v).astype(BF16)


def _inproj(x2, g, w, b, gq, gk, grp, tri, bound, t):
    n = x2.shape[0]
    tm = TM_PROJ
    aug_w = A_HEADS * LANE
    const = lambda i: (0, 0)
    row = lambda width: pl.BlockSpec((tm, width), lambda i: (i, 0))
    return pl.pallas_call(
        functools.partial(_inproj_kernel, tiles_per_seq=t // tm),
        grid=(n // tm,),
        in_specs=[
            row(D_MODEL),
            pl.BlockSpec((1, D_MODEL), const),
            pl.BlockSpec((D_MODEL, IN_W), const),
            pl.BlockSpec((1, IN_W), const),
            pl.BlockSpec((1, A_DIM), const),
            pl.BlockSpec((1, A_DIM), const),
            pl.BlockSpec((A_DIM, A_DIM), const),
            pl.BlockSpec((tm, tm), const),
            pl.BlockSpec((1, LANE), const),
        ],
        out_specs=[row(aug_w), row(aug_w), row(aug_w), row(REST_W), row(LANE)],
        out_shape=[
            jax.ShapeDtypeStruct((n, aug_w), BF16),
            jax.ShapeDtypeStruct((n, aug_w), BF16),
            jax.ShapeDtypeStruct((n, aug_w), BF16),
            jax.ShapeDtypeStruct((n, REST_W), BF16),
            jax.ShapeDtypeStruct((n, LANE), F32),
        ],
        scratch_shapes=[pltpu.VMEM((1, LANE), F32)],
        compiler_params=pltpu.CompilerParams(dimension_semantics=("arbitrary",), vmem_limit_bytes=VMEM_LIMIT),
        name="inproj",
    )(x2, g, w, b, gq, gk, grp, tri, bound)


def _fox_kernel(first_ref, q_ref, k_ref, v_ref, o_ref, *, tq):
    i = pl.program_id(2)
    step = (pl.program_id(0) * pl.num_programs(1) + pl.program_id(1)) * pl.num_programs(2) + i
    half = tq // 2
    nt = (((1,), (1,)), ((), ()))

    def attend(q, keys, hh, visible=None):
        cols = slice(hh * LANE, (hh + 1) * LANE)
        s = lax.dot_general(q, k_ref[keys, cols], nt, preferred_element_type=F32)
        if visible is not None:
            s = jnp.where(visible, s, NEG_INF)
        return jnp.dot(jnp.exp2(s).astype(BF16), v_ref[keys, cols], preferred_element_type=F32)

    def full_step(j, accs):
        keys = pl.ds(pl.multiple_of(j * tq, tq), tq)
        return tuple(accs[hh] + attend(q_ref[:, hh * LANE:(hh + 1) * LANE], keys, hh) for hh in range(2))

    zero = jnp.zeros((tq, LANE), F32)
    accs = lax.fori_loop(first_ref[step], i, full_step, (zero, zero))

    off = pl.multiple_of(i * tq, tq)
    row_t = lax.broadcasted_iota(jnp.int32, (half, half), 0)
    col_t = lax.broadcasted_iota(jnp.int32, (half, half), 1)
    row_b = lax.broadcasted_iota(jnp.int32, (half, tq), 0) + half
    col_b = lax.broadcasted_iota(jnp.int32, (half, tq), 1)
    lane = lax.broadcasted_iota(jnp.int32, (half, LANE), 1)
    outs = []
    for hh in range(2):
        cols = slice(hh * LANE, (hh + 1) * LANE)
        top = accs[hh][:half] + attend(q_ref[:half, cols], pl.ds(off, half), hh, col_t <= row_t)
        bot = accs[hh][half:] + attend(q_ref[half:, cols], pl.ds(off, tq), hh, col_b <= row_b)
        outs.append((top / top[:, HEAD_DIM:HEAD_DIM + 1], bot / bot[:, HEAD_DIM:HEAD_DIM + 1]))
    for r, rows in enumerate((slice(0, half), slice(half, tq))):
        o_ref[rows, :] = jnp.where(lane < HEAD_DIM, outs[0][r], pltpu.roll(outs[1][r], HEAD_DIM, 1)).astype(BF16)


def _fox(qa, ka, va, bsz, t):
    n = qa.shape[0]
    tq = TQ_FOX
    nq = t // tq
    pairs = A_HEADS // 2

    k_tiles = ka.reshape(bsz, nq, tq, A_HEADS, LANE)
    c_of = lambda rows: -jnp.sum(rows[..., HEAD_DIM:HEAD_DIM + 3].astype(F32), axis=-1)
    gap = c_of(k_tiles[:, :, 0])[:, :, None, :] - c_of(k_tiles[:, :, tq - 1])[:, None, :, :]
    dead = jnp.all((gap < -FOX_SKIP).reshape(bsz, nq, nq, pairs, 2), axis=-1)
    dead = dead & (jnp.arange(nq)[None, :, None, None] > jnp.arange(nq)[None, None, :, None])
    first = jnp.sum(dead, axis=2).astype(jnp.int32).transpose(0, 2, 1).reshape(-1)

    return pl.pallas_call(
        functools.partial(_fox_kernel, tq=tq),
        grid_spec=pltpu.PrefetchScalarGridSpec(
            num_scalar_prefetch=1,
            grid=(bsz, pairs, nq),
            in_specs=[
                pl.BlockSpec((tq, 2 * LANE), lambda b, p, i, first: (b * nq + i, p)),
                pl.BlockSpec((t, 2 * LANE), lambda b, p, i, first: (b, p)),
                pl.BlockSpec((t, 2 * LANE), lambda b, p, i, first: (b, p)),
            ],
            out_specs=pl.BlockSpec((tq, LANE), lambda b, p, i, first: (b * nq + i, p)),
        ),
        out_shape=jax.ShapeDtypeStruct((n, A_DIM), BF16),
        compiler_params=pltpu.CompilerParams(
            dimension_semantics=("arbitrary", "arbitrary", "arbitrary"), vmem_limit_bytes=VMEM_LIMIT),
        name="fox",
    )(first, qa, ka, va)


def _fox_safe_kernel(q_ref, k_ref, v_ref, c_ref, o_ref, *, tq):
    i = pl.program_id(2)
    q = q_ref[...]
    lane = lax.broadcasted_iota(jnp.int32, (tq, LANE), 1)
    zero = jnp.zeros_like(q)
    qh = (jnp.where(lane < HEAD_DIM, q, zero), jnp.where(lane >= HEAD_DIM, q, zero))
    tri = lax.broadcasted_iota(jnp.int32, (tq, tq), 1) <= lax.broadcasted_iota(jnp.int32, (tq, tq), 0)

    def step(j, carry, masked):
        off = pl.multiple_of(j * tq, tq)
        kt = k_ref[pl.ds(off, tq), :]
        vt = v_ref[pl.ds(off, tq), :]
        new = []
        for hh in range(2):
            m, l, acc = carry[hh]
            s = lax.dot_general(qh[hh], kt, (((1,), (1,)), ((), ())), preferred_element_type=F32)
            s = s - c_ref[0, hh:hh + 1, pl.ds(off, tq)]
            if masked:
                s = jnp.where(tri, s, NEG_INF)
            m_new = jnp.maximum(m, jnp.max(s, axis=-1, keepdims=True))
            alpha = jnp.exp2(m - m_new)
            p = jnp.exp2(s - m_new)
            l = alpha * l + jnp.sum(p, axis=-1, keepdims=True)
            acc = alpha * acc + jnp.dot(p.astype(BF16), vt, preferred_element_type=F32)
            new.append((m_new, l, acc))
        return tuple(new)

    init = tuple((jnp.full((tq, 1), NEG_INF, F32), jnp.zeros((tq, 1), F32), jnp.zeros((tq, LANE), F32))
                 for _ in range(2))
    carry = lax.fori_loop(0, i, lambda j, c: step(j, c, False), init)
    (_, l0, a0), (_, l1, a1) = step(i, carry, True)
    o_ref[...] = jnp.where(lane < HEAD_DIM, a0 / l0, a1 / l1).astype(BF16)


def _fox_safe(qa, ka, va, bsz, t):
    n = qa.shape[0]
    tq = TQ_FOX // 2
    nq = t // tq
    pairs = A_HEADS // 2
    heads = lambda z: z.reshape(n, A_HEADS, LANE)
    unpack = lambda z: heads(z)[:, :, :HEAD_DIM].reshape(n, A_DIM)
    c = -jnp.sum(heads(ka)[:, :, HEAD_DIM:HEAD_DIM + 3].astype(F32), axis=-1)
    c_t = c.reshape(bsz, t, A_HEADS).transpose(0, 2, 1).reshape(bsz * pairs, 2, t)
    return pl.pallas_call(
        functools.partial(_fox_safe_kernel, tq=tq),
        grid=(bsz, pairs, nq),
        in_specs=[
            pl.BlockSpec((tq, LANE), lambda b, p, i: (b * nq + i, p)),
            pl.BlockSpec((t, LANE), lambda b, p, i: (b, p)),
            pl.BlockSpec((t, LANE), lambda b, p, i: (b, p)),
            pl.BlockSpec((1, 2, t), lambda b, p, i: (b * pairs + p, 0, 0)),
        ],
        out_specs=pl.BlockSpec((tq, LANE), lambda b, p, i: (b * nq + i, p)),
        out_shape=jax.ShapeDtypeStruct((n, A_DIM), BF16),
        compiler_params=pltpu.CompilerParams(
            dimension_semantics=("arbitrary", "arbitrary", "arbitrary"), vmem_limit_bytes=VMEM_LIMIT),
        name="fox_safe",
    )(unpack(qa), unpack(ka), unpack(va), c_t)


def _mlstm_kernel(qk_ref, halo_ref, v_ref, gates_ref, cw_ref, tri_ref, triu_ref, grp_ref, spread_ref, o_ref,
                  buf_ref, c_st, n_st, m_st, *, tc, chunk):
    j = pl.program_id(1)
    width = B_DIM

    @pl.when(j == 0)
    def _():
        c_st[...] = jnp.zeros_like(c_st)
        n_st[...] = jnp.zeros_like(n_st)
        m_st[...] = jnp.full(m_st.shape, NEG_INF, F32)

    hr = BF16_ROWS
    buf_ref[0:hr, :] = jnp.where(j == 0, 0.0, halo_ref[...].astype(F32))
    buf_ref[hr:hr + tc, :] = qk_ref[...].astype(F32)
    cw = cw_ref[...]
    y = cw[MLSTM_CONV - 1:MLSTM_CONV] * buf_ref[hr:hr + tc, :]
    for d in range(1, MLSTM_CONV):
        y = y + cw[MLSTM_CONV - 1 - d:MLSTM_CONV - d] * buf_ref[hr - d:hr - d + tc, :]
    y = y * _sigmoid(y)
    q_all = y[:, :width]
    k_all = y[:, width:] * (HEAD_DIM ** -0.5)

    head = _head_of_lane((chunk, width))
    head_sq = _head_of_lane((width, width))
    blockdiag = head_sq == lax.broadcasted_iota(jnp.int32, (width, width), 0) // HEAD_DIM
    causal = lax.broadcasted_iota(jnp.int32, (chunk, chunk), 1) <= lax.broadcasted_iota(jnp.int32, (chunk, chunk), 0)
    tri = tri_ref[...]
    grp = grp_ref[...]

    gx = None
    for part in _split3(gates_ref[...]):
        d = jnp.dot(part.astype(BF16), spread_ref[...], preferred_element_type=F32)
        gx = d if gx is None else gx + d
    i_all = gx[:, :width]
    f_all = gx[:, width:]

    for c in range(tc // chunk):
        r0 = c * chunk
        q = q_all[r0:r0 + chunk]
        k = k_all[r0:r0 + chunk]
        vb = v_ref[r0:r0 + chunk, :]
        qb, kb = q.astype(BF16), k.astype(BF16)
        i_x = i_all[r0:r0 + chunk]
        bc_x = jnp.dot(tri, _log_sigmoid(f_all[r0:r0 + chunk]), precision=HIGHEST,
                       preferred_element_type=F32)
        c_prev, n_prev, m_prev = c_st[...], n_st[...], m_st[...]
        g_t = gates_ref[r0:r0 + chunk, :].T[A_HEADS:A_HEADS + 2 * B_HEADS, :]
        bc_t = jnp.dot(_log_sigmoid(g_t), triu_ref[...], precision=HIGHEST, preferred_element_type=F32)

        m_cols, den_cols = [], []
        num_x = jnp.zeros((chunk, width), F32)
        for h in range(B_HEADS):
            lo = h * HEAD_DIM
            bc_col = bc_x[:, lo:lo + 1]
            row = g_t[h:h + 1, :] - bc_t[B_HEADS + h:B_HEADS + h + 1, :]
            log_d = jnp.where(causal, bc_col + row, NEG_INF)
            m_out = jnp.maximum(bc_col + m_prev[:, lo:lo + 1], jnp.max(log_d, axis=-1, keepdims=True))
            s = lax.dot_general(jnp.where(head == h, qb, jnp.zeros_like(qb)), kb, (((1,), (1,)), ((), ())),
                                preferred_element_type=F32)
            sqk = s * jnp.exp(log_d - m_out)
            num_x = jnp.where(head == h, jnp.dot(sqk.astype(BF16), vb, preferred_element_type=F32), num_x)
            m_cols.append(m_out)
            den_cols.append(jnp.sum(sqk, axis=-1, keepdims=True))
        m_out_x = _expand_heads(m_cols, head)
        den_x = _expand_heads(den_cols, head)

        inter_w = jnp.exp(bc_x + m_prev - m_out_x)
        q_c = jnp.dot(qb, c_prev.astype(BF16), preferred_element_type=F32)
        q_n = jnp.dot(q * n_prev, grp, precision=HIGHEST, preferred_element_type=F32)
        num = num_x + inter_w * q_c
        den = den_x + inter_w * q_n
        o_ref[r0:r0 + chunk, :] = (num / jnp.maximum(jnp.abs(den), jnp.exp(-m_out_x))).astype(BF16)

        g_x = bc_x[chunk - 1:chunk, :]
        a_x = g_x - bc_x + i_x
        m_new = jnp.maximum(g_x + m_prev, jnp.max(a_x, axis=0, keepdims=True))
        decay = jnp.exp(g_x + m_prev - m_new)
        kw = k * jnp.exp(a_x - m_new)
        upd = lax.dot_general(kw.astype(BF16), vb, (((0,), (0,)), ((), ())), preferred_element_type=F32)
        c_st[...] = decay * c_prev + jnp.where(blockdiag, upd, 0.0)
        n_st[...] = decay * n_prev + jnp.sum(kw, axis=0, keepdims=True)
        m_st[...] = m_new


def _mlstm(rest, gates, conv_w, tri_l, tri_u, grp, spread, bsz, t):
    n = rest.shape[0]
    tc, nt = TC_MLSTM, t // TC_MLSTM
    hr = BF16_ROWS
    const = lambda b, j: (0, 0)
    return pl.pallas_call(
        functools.partial(_mlstm_kernel, tc=tc, chunk=L_MLSTM),
        grid=(bsz, nt),
        in_specs=[
            pl.BlockSpec((tc, 2 * B_DIM), lambda b, j: (b * nt + j, 0)),
            pl.BlockSpec((hr, 2 * B_DIM), lambda b, j: (jnp.maximum((b * nt + j) * (tc // hr) - 1, 0), 0)),
            pl.BlockSpec((tc, B_DIM), lambda b, j: (b * nt + j, 2)),
            pl.BlockSpec((tc, LANE), lambda b, j: (b * nt + j, 0)),
            pl.BlockSpec((MLSTM_CONV, 2 * B_DIM), const),
            pl.BlockSpec((L_MLSTM, L_MLSTM), const),
            pl.BlockSpec((L_MLSTM, L_MLSTM), const),
            pl.BlockSpec((B_DIM, B_DIM), const),
            pl.BlockSpec((LANE, 2 * B_DIM), const),
        ],
        out_specs=pl.BlockSpec((tc, B_DIM), lambda b, j: (b * nt + j, 0)),
        out_shape=jax.ShapeDtypeStruct((n, B_DIM), BF16),
        scratch_shapes=[
            pltpu.VMEM((tc + hr, 2 * B_DIM), F32),
            pltpu.VMEM((B_DIM, B_DIM), F32),
            pltpu.VMEM((1, B_DIM), F32),
            pltpu.VMEM((1, B_DIM), F32),
        ],
        compiler_params=pltpu.CompilerParams(dimension_semantics=("arbitrary", "arbitrary"),
                                             vmem_limit_bytes=VMEM_LIMIT),
        name="mlstm",
    )(rest, rest, rest, gates, conv_w, tri_l, tri_u, grp, spread)


def _hgrn_kernel(q_ref, f_ref, i_ref, la_ref, l1m_ref, tri_ref, tot_ref, o_ref, st_ref, *, tc, chunk):
    nb = q_ref.shape[0]
    width = C_DIM

    @pl.when(pl.program_id(0) == 0)
    def _():
        st_ref[...] = jnp.zeros_like(st_ref)

    head = _head_of_lane((tc, width))
    head_sq = _head_of_lane((width, width))
    blockdiag = head_sq == lax.broadcasted_iota(jnp.int32, (width, width), 0) // HEAD_DIM
    rr = lax.broadcasted_iota(jnp.int32, (tc, tc), 0)
    cc = lax.broadcasted_iota(jnp.int32, (tc, tc), 1)
    keep = (cc <= rr) & (cc // chunk == rr // chunk)

    seqs = []
    for s in range(nb):
        xq = q_ref[s].astype(F32)
        q = xq * _sigmoid(xq)
        xf = f_ref[s].astype(F32)
        ls = _log_sigmoid(xf)
        a = la_ref[...]
        b = l1m_ref[...] + ls
        logf = jnp.maximum(a, b) + jnp.log1p(jnp.exp(-jnp.abs(a - b)))
        kk = jnp.exp(l1m_ref[...] + ls - xf)
        vb = i_ref[s]

        bc = jnp.dot(tri_ref[...], logf, precision=HIGHEST, preferred_element_type=F32)
        blast = jnp.dot(tot_ref[...], logf, precision=HIGHEST, preferred_element_type=F32)
        qt = (q * jnp.exp(bc)).astype(BF16)
        kt = (kk * jnp.exp(-bc)).astype(BF16)
        kd = (kk * jnp.exp(blast - bc)).astype(BF16)

        intra = jnp.zeros((tc, width), F32)
        for h in range(C_HEADS):
            att = lax.dot_general(jnp.where(head == h, qt, jnp.zeros_like(qt)), kt, (((1,), (1,)), ((), ())),
                                  preferred_element_type=F32)
            att = jnp.where(keep, att, 0.0).astype(BF16)
            intra = jnp.where(head == h, jnp.dot(att, vb, preferred_element_type=F32), intra)
        seqs.append((qt, kd, vb, jnp.exp(blast), intra))

    for c in range(tc // chunk):
        r0 = c * chunk
        for s, (qt, kd, vb, chunk_decay, intra) in enumerate(seqs):
            s_t = st_ref[s]
            inter = lax.dot_general(qt[r0:r0 + chunk], s_t.astype(BF16), (((1,), (1,)), ((), ())),
                                    preferred_element_type=F32)
            o_ref[s, r0:r0 + chunk, :] = (intra[r0:r0 + chunk] + inter).astype(BF16)
            upd = lax.dot_general(vb[r0:r0 + chunk], kd[r0:r0 + chunk], (((0,), (0,)), ((), ())),
                                  preferred_element_type=F32)
            st_ref[s] = s_t * chunk_decay[r0:r0 + 1, :] + jnp.where(blockdiag, upd, 0.0)


def _hgrn(rest, log_lb, log_1m_lb, tri_bd, tot_bd, bsz, t):
    tc, nt = TC_HGRN, t // TC_HGRN
    rest3 = rest.reshape(bsz, t, REST_W)
    blk = lambda col: pl.BlockSpec((bsz, tc, C_DIM), lambda j: (0, j, col))
    const = lambda j: (0, 0)
    out = pl.pallas_call(
        functools.partial(_hgrn_kernel, tc=tc, chunk=L_HGRN),
        grid=(nt,),
        in_specs=[
            blk(4), blk(5), blk(6),
            pl.BlockSpec((1, C_DIM), const),
            pl.BlockSpec((1, C_DIM), const),
            pl.BlockSpec((tc, tc), const),
            pl.BlockSpec((tc, tc), const),
        ],
        out_specs=pl.BlockSpec((bsz, tc, C_DIM), lambda j: (0, j, 0)),
        out_shape=jax.ShapeDtypeStruct((bsz, t, C_DIM), BF16),
        scratch_shapes=[pltpu.VMEM((bsz, C_DIM, C_DIM), F32)],
        compiler_params=pltpu.CompilerParams(dimension_semantics=("arbitrary",), vmem_limit_bytes=VMEM_LIMIT),
        name="hgrn2",
    )(rest3, rest3, rest3, log_lb, log_1m_lb, tri_bd, tot_bd)
    return out.reshape(bsz * t, C_DIM)


def _outproj_kernel(x_ref, oa_ref, ob_ref, oc_ref, bo_ref, cg_ref, g_ref, grp_ref, w_ref, o_ref):
    def normed(o, lo, hi):
        return o * lax.rsqrt(_group_mean_sq(o, grp_ref) + EPS) * g_ref[:, lo:hi]

    ya = normed(oa_ref[...].astype(F32), 0, A_DIM)
    yb = _sigmoid(bo_ref[...].astype(F32)) * normed(ob_ref[...].astype(F32), A_DIM, A_DIM + B_DIM)
    cg = cg_ref[...].astype(F32)
    yc = cg * _sigmoid(cg) * normed(oc_ref[...].astype(F32), A_DIM + B_DIM, D_MIX)
    acc = x_ref[...] + jnp.dot(ya.astype(BF16), w_ref[0:A_DIM, :], preferred_element_type=F32)
    acc = acc + jnp.dot(yb.astype(BF16), w_ref[A_DIM:A_DIM + B_DIM, :], preferred_element_type=F32)
    acc = acc + jnp.dot(yc.astype(BF16), w_ref[A_DIM + B_DIM:D_MIX, :], preferred_element_type=F32)
    o_ref[...] = acc


def _outproj(x2, oa, ob, oc, rest, out_g, grp, w_out):
    n = x2.shape[0]
    tm = TM_PROJ
    const = lambda i: (0, 0)
    return pl.pallas_call(
        _outproj_kernel,
        grid=(n // tm,),
        in_specs=[
            pl.BlockSpec((tm, D_MODEL), lambda i: (i, 0)),
            pl.BlockSpec((tm, A_DIM), lambda i: (i, 0)),
            pl.BlockSpec((tm, B_DIM), lambda i: (i, 0)),
            pl.BlockSpec((tm, C_DIM), lambda i: (i, 0)),
            pl.BlockSpec((tm, B_DIM), lambda i: (i, 3)),
            pl.BlockSpec((tm, C_DIM), lambda i: (i, 7)),
            pl.BlockSpec((1, D_MIX), const),
            pl.BlockSpec((A_DIM, A_DIM), const),
            pl.BlockSpec((D_MIX, D_MODEL), const),
        ],
        out_specs=pl.BlockSpec((tm, D_MODEL), lambda i: (i, 0)),
        out_shape=jax.ShapeDtypeStruct((n, D_MODEL), F32),
        compiler_params=pltpu.CompilerParams(dimension_semantics=("arbitrary",), vmem_limit_bytes=VMEM_LIMIT),
        name="outproj",
    )(x2, oa, ob, oc, rest, rest, out_g, grp, w_out)


def _ffn_kernel(x_ref, halo_ref, g_ref, wup_ref, cw_ref, cb_ref, wdn_ref, o_ref, h_ref, ug_ref, uu_ref, act_ref,
                *, tm, tiles_per_seq):
    i = pl.program_id(0)
    halo_rows = BF16_ROWS

    def rms(x):
        ms = jnp.mean(x * x, axis=-1, keepdims=True)
        return x * lax.rsqrt(ms + EPS) * g_ref[...]

    x = x_ref[...]
    first = (i % tiles_per_seq) == 0
    h_ref[0:halo_rows, :] = jnp.where(first, 0.0, rms(halo_ref[...])).astype(BF16)
    h_ref[halo_rows:, :] = rms(x).astype(BF16)
    h = h_ref[...]

    def conv(u_ref, slot, lo):
        y = cb_ref[:, lo:lo + FF_CHUNK] + cw_ref[FFN_CONV - 1:FFN_CONV, lo:lo + FF_CHUNK] * u_ref[slot, halo_rows:, :]
        for d in range(1, FFN_CONV):
            y = y + cw_ref[FFN_CONV - 1 - d:FFN_CONV - d, lo:lo + FF_CHUNK] * u_ref[slot, pl.ds(halo_rows - d, tm), :]
        return y

    def up_proj(c):
        lo = c * FF_CHUNK
        ug_ref[c % 2] = jnp.dot(h, wup_ref[:, lo:lo + FF_CHUNK], preferred_element_type=F32)
        uu_ref[c % 2] = jnp.dot(h, wup_ref[:, D_FF + lo:D_FF + lo + FF_CHUNK], preferred_element_type=F32)

    n_chunks = D_FF // FF_CHUNK
    up_proj(0)
    for c in range(n_chunks):
        if c + 1 < n_chunks:
            up_proj(c + 1)
        lo = c * FF_CHUNK
        gate = conv(ug_ref, c % 2, lo)
        up = conv(uu_ref, c % 2, D_FF + lo)
        act_ref[:, lo:lo + FF_CHUNK] = (gate * _sigmoid(gate) * up).astype(BF16)
    o_ref[...] = x + jnp.dot(act_ref[...], wdn_ref[...], preferred_element_type=F32)


def _ffn(x2, g, w_up, conv_w, conv_b, w_down, t):
    n = x2.shape[0]
    tm = TM_PROJ
    halo = BF16_ROWS
    const = lambda i: (0, 0)
    return pl.pallas_call(
        functools.partial(_ffn_kernel, tm=tm, tiles_per_seq=t // tm),
        grid=(n // tm,),
        in_specs=[
            pl.BlockSpec((tm, D_MODEL), lambda i: (i, 0)),
            pl.BlockSpec((halo, D_MODEL), lambda i: (jnp.maximum(i * (tm // halo) - 1, 0), 0)),
            pl.BlockSpec((1, D_MODEL), const),
            pl.BlockSpec((D_MODEL, 2 * D_FF), const),
            pl.BlockSpec((FFN_CONV, 2 * D_FF), const),
            pl.BlockSpec((1, 2 * D_FF), const),
            pl.BlockSpec((D_FF, D_MODEL), const),
        ],
        out_specs=pl.BlockSpec((tm, D_MODEL), lambda i: (i, 0)),
        out_shape=jax.ShapeDtypeStruct((n, D_MODEL), F32),
        scratch_shapes=[
            pltpu.VMEM((tm + halo, D_MODEL), BF16),
            pltpu.VMEM((2, tm + halo, FF_CHUNK), F32),
            pltpu.VMEM((2, tm + halo, FF_CHUNK), F32),
            pltpu.VMEM((tm, D_FF), BF16),
        ],
        compiler_params=pltpu.CompilerParams(dimension_semantics=("arbitrary",), vmem_limit_bytes=VMEM_LIMIT),
        name="ffn",
    )(x2, x2, g, w_up, conv_w, conv_b, w_down)


def _block_diag_ones(n, blk, dtype):
    r = np.arange(n) // blk
    return jnp.asarray(r[:, None] == r[None, :], dtype)


def kernel(x, lb_logits, norm_mix_g, w_in, b_in, a_q_g, a_k_g, b_conv_w, out_g, w_out, norm_ffn_g, w_up,
           ffn_conv_w, ffn_conv_b, w_down):
    bsz, t, d = x.shape
    depth = w_in.shape[0]
    assert d == D_MODEL and t % TQ_FOX == 0 and t % TC_MLSTM == 0 and t % TC_HGRN == 0
    n = bsz * t
    x2 = x.reshape(n, d).astype(F32)

    p = jax.nn.softmax(lb_logits.astype(F32), axis=0)
    lb_all = jnp.maximum(jnp.cumsum(p, axis=0) - p[0], 0.0)
    log_lb = jnp.log(lb_all)
    log_1m_lb = jnp.log1p(-lb_all)

    grp_a = _block_diag_ones(A_DIM, HEAD_DIM, BF16)
    grp_b = _block_diag_ones(B_DIM, HEAD_DIM, F32)
    ar = np.arange(L_MLSTM)
    tri_u = jnp.asarray(ar[:, None] <= ar[None, :], F32)
    tri_l = jnp.asarray(ar[:, None] >= ar[None, :], F32)
    gate_row = np.arange(LANE)[:, None] - A_HEADS
    spread = jnp.asarray(gate_row == np.arange(2 * B_DIM)[None, :] // HEAD_DIM, BF16)
    ap = np.arange(TM_PROJ)
    tri_proj = jnp.asarray(ap[:, None] >= ap[None, :], BF16)
    ah = np.arange(TC_HGRN)
    same = (ah[:, None] // L_HGRN) == (ah[None, :] // L_HGRN)
    tri_bd = jnp.asarray(same & (ah[:, None] >= ah[None, :]), F32)
    tot_bd = jnp.asarray(same, F32)

    for l in range(depth):
        w_r = _reorder_in_cols(w_in[l].astype(BF16))
        b_r = _reorder_in_cols(b_in[l].astype(F32))[None, :]
        gq = (jnp.tile(a_q_g[l].astype(F32), A_HEADS) * (HEAD_DIM ** -0.5 * LOG2E))[None, :]
        gk = jnp.tile(a_k_g[l].astype(F32), A_HEADS)[None, :]
        bound = HEAD_DIM * jnp.max(jnp.abs(gq)) * jnp.max(jnp.abs(gk))
        qa, ka, va, rest, gates = _inproj(x2, norm_mix_g[l][None, :].astype(F32), w_r, b_r, gq, gk, grp_a, tri_proj,
                                          jnp.full((1, LANE), bound, F32), t)
        oa = lax.cond(bound < FOX_BOUND_MAX, functools.partial(_fox, bsz=bsz, t=t),
                      functools.partial(_fox_safe, bsz=bsz, t=t), qa, ka, va)
        ob = _mlstm(rest, gates, b_conv_w[l].astype(F32), tri_l, tri_u, grp_b, spread, bsz, t)
        oc = _hgrn(rest, log_lb[l][None, :], log_1m_lb[l][None, :], tri_bd, tot_bd, bsz, t)
        x2 = _outproj(x2, oa, ob, oc, rest, out_g[l][None, :].astype(F32), grp_a, w_out[l].astype(BF16))
        x2 = _ffn(x2, norm_ffn_g[l][None, :].astype(F32), w_up[l].astype(BF16), ffn_conv_w[l].astype(F32),
                  ffn_conv_b[l][None, :].astype(F32), w_down[l].astype(BF16), t)
    return x2.reshape(bsz, t, d).astype(x.dtype)
```

```python
import functools

import numpy as np
import jax
import jax.numpy as jnp
from jax import lax
from jax.experimental import pallas as pl
from jax.experimental.pallas import tpu as pltpu

F32 = jnp.float32
BF16 = jnp.bfloat16
NEG_INF = float("-inf")
LOG2E = 1.4426950408889634

D_MODEL = 1024
HEAD_DIM = 64
A_HEADS, B_HEADS, C_HEADS = 8, 4, 4
A_DIM, B_DIM, C_DIM = A_HEADS * HEAD_DIM, B_HEADS * HEAD_DIM, C_HEADS * HEAD_DIM
D_MIX = A_DIM + B_DIM + C_DIM
D_FF = 2816
MLSTM_CONV = 4
FFN_CONV = 3
EPS = 1e-6

LANE = 128
BF16_ROWS = 16
VMEM_LIMIT = 56 * 1024 * 1024

QKV_W = 3 * A_DIM
REST_W = 8 * B_DIM
IN_W = QKV_W + REST_W + LANE

TM_PROJ = 512
TQ_FOX = 1024
FOX_BOUND_MAX = 40.0
FOX_SKIP = 160.0
TC_MLSTM = 512
L_MLSTM = 128
TC_HGRN = 256
L_HGRN = 16
FF_CHUNK = 256


def _reorder_in_cols(w):
    a_f = 3 * A_DIM
    b_qk = a_f + A_HEADS
    b_v = b_qk + 2 * B_DIM
    b_i = b_v + B_DIM
    b_f = b_i + B_HEADS
    b_o = b_f + B_HEADS
    c_q = b_o + B_DIM
    end = c_q + 4 * C_DIM
    sl = lambda lo, hi: lax.slice_in_dim(w, lo, hi, axis=-1)
    n_gate = A_HEADS + 2 * B_HEADS
    pad = jnp.zeros(w.shape[:-1] + (LANE - n_gate,), w.dtype)
    out = jnp.concatenate([sl(0, a_f), sl(b_qk, b_i), sl(b_o, end), sl(a_f, b_qk), sl(b_i, b_o), pad], axis=-1)
    assert out.shape[-1] == IN_W
    return out


def _log_sigmoid(x):
    return jnp.minimum(x, 0.0) - jnp.log(1.0 + jnp.exp(-jnp.abs(x)))


def _sigmoid(x):
    return 1.0 / (1.0 + jnp.exp(-x))


def _head_of_lane(shape):
    return lax.broadcasted_iota(jnp.int32, shape, len(shape) - 1) // HEAD_DIM


def _expand_heads(cols, head):
    out = cols[-1]
    for h in range(len(cols) - 2, -1, -1):
        out = jnp.where(head == h, cols[h], out)
    return out


def _group_mean_sq(z, grp_ref):
    w = z.shape[-1]
    zz = (z * z).astype(BF16)
    return jnp.dot(zz, grp_ref[0:w, 0:w], preferred_element_type=F32) * (1.0 / HEAD_DIM)


def _split3(x):
    hi = x.astype(BF16).astype(F32)
    mid = (x - hi).astype(BF16).astype(F32)
    lo = (x - hi - mid).astype(BF16).astype(F32)
    return hi, mid, lo


def _select_sum(sel, x, parts=3):
    out = None
    for term in _split3(x)[:parts]:
        d = jnp.dot(sel, term.astype(BF16), preferred_element_type=F32)
        out = d if out is None else out + d
    return out


def _sum_select(x, sel, parts=3):
    out = None
    for term in _split3(x)[:parts]:
        d = jnp.dot(term.astype(BF16), sel, preferred_element_type=F32)
        out = d if out is None else out + d
    return out


def _inproj_kernel(x_ref, g_ref, w_ref, b_ref, gq_ref, gk_ref, grp_ref, tri_ref, bound_ref,
                   qa_ref, ka_ref, va_ref, rest_ref, gates_ref, edges_ref, carry_ref, *, tiles_per_seq):
    i = pl.program_id(0)
    tm = x_ref.shape[0]

    @pl.when(i % tiles_per_seq == 0)
    def _():
        carry_ref[...] = jnp.zeros_like(carry_ref)

    x = x_ref[...]
    ms = jnp.mean(x * x, axis=-1, keepdims=True)
    h = (x * lax.rsqrt(ms + EPS) * g_ref[...]).astype(BF16)

    def proj(lo, hi):
        return jnp.dot(h, w_ref[:, lo:hi], preferred_element_type=F32) + b_ref[:, lo:hi]

    gates = proj(QKV_W + REST_W, IN_W)
    gates_ref[...] = gates
    q = proj(0, A_DIM)
    k = proj(A_DIM, 2 * A_DIM)
    v = proj(2 * A_DIM, QKV_W)

    cs = _select_sum(tri_ref[...], _log_sigmoid(gates))
    c2 = (cs + carry_ref[...]) * LOG2E
    carry_ref[...] = carry_ref[...] + cs[tm - 1:tm, :]
    edge_row = lax.broadcasted_iota(jnp.int32, (8, LANE), 0)
    edges_ref[0] = jnp.where(edge_row == 0, c2[0:1, :], jnp.where(edge_row == 1, c2[tm - 1:tm, :], 0.0))

    q = q * lax.rsqrt(_group_mean_sq(q, grp_ref) + EPS) * gq_ref[...]
    k = k * lax.rsqrt(_group_mean_sq(k, grp_ref) + EPS) * gk_ref[...]
    for c in range(REST_W // A_DIM):
        rest_ref[:, c * A_DIM:(c + 1) * A_DIM] = proj(QKV_W + c * A_DIM, QKV_W + (c + 1) * A_DIM).astype(BF16)

    lane = lax.broadcasted_iota(jnp.int32, (tm, LANE), 1)
    ones_q = jnp.where((lane >= HEAD_DIM) & (lane < HEAD_DIM + 3), 1.0, 0.0)
    ones_k = jnp.where((lane >= HEAD_DIM + 3) & (lane < HEAD_DIM + 6), 1.0, 0.0)
    ones_v = jnp.where(lane == HEAD_DIM, 1.0, 0.0)
    data = lane < HEAD_DIM
    for hd in range(A_HEADS):
        pair = slice((hd // 2) * LANE, (hd // 2 + 1) * LANE)
        out = slice(hd * LANE, (hd + 1) * LANE)
        place = (lambda z: z) if hd % 2 == 0 else (lambda z: pltpu.roll(z, HEAD_DIM, 1))
        cb = jnp.broadcast_to(c2[:, hd:hd + 1], (tm, LANE))
        qh, qm, ql = _split3(cb - bound_ref[...])
        kh, km, kl = _split3(cb)
        aug_q = jnp.where(lane == HEAD_DIM + 3, qh, jnp.where(lane == HEAD_DIM + 4, qm,
                          jnp.where(lane == HEAD_DIM + 5, ql, ones_q)))
        aug_k = jnp.where(lane == HEAD_DIM, -kh, jnp.where(lane == HEAD_DIM + 1, -km,
                          jnp.where(lane == HEAD_DIM + 2, -kl, ones_k)))
        qa_ref[:, out] = jnp.where(data, place(q[:, pair]), aug_q).astype(BF16)
        ka_ref[:, out] = jnp.where(data, place(k[:, pair]), aug_k).astype(BF16)
        va_ref[:, out] = jnp.where(data, place(v[:, pair]), ones_v).astype(BF16)


def _inproj(x2, g, w, b, gq, gk, grp, tri, bound, t):
    n = x2.shape[0]
    tm = TM_PROJ
    aug_w = A_HEADS * LANE
    const = lambda i: (0, 0)
    row = lambda width: pl.BlockSpec((tm, width), lambda i: (i, 0))
    return pl.pallas_call(
        functools.partial(_inproj_kernel, tiles_per_seq=t // tm),
        grid=(n // tm,),
        in_specs=[
            row(D_MODEL),
            pl.BlockSpec((1, D_MODEL), const),
            pl.BlockSpec((D_MODEL, IN_W), const),
            pl.BlockSpec((1, IN_W), const),
            pl.BlockSpec((1, A_DIM), const),
            pl.BlockSpec((1, A_DIM), const),
            pl.BlockSpec((A_DIM, A_DIM), const),
            pl.BlockSpec((tm, tm), const),
            pl.BlockSpec((1, LANE), const),
        ],
        out_specs=[row(aug_w), row(aug_w), row(aug_w), row(REST_W), row(LANE),
                   pl.BlockSpec((1, 8, LANE), lambda i: (i, 0, 0))],
        out_shape=[
            jax.ShapeDtypeStruct((n, aug_w), BF16),
            jax.ShapeDtypeStruct((n, aug_w), BF16),
            jax.ShapeDtypeStruct((n, aug_w), BF16),
            jax.ShapeDtypeStruct((n, REST_W), BF16),
            jax.ShapeDtypeStruct((n, LANE), F32),
            jax.ShapeDtypeStruct((n // tm, 8, LANE), F32),
        ],
        scratch_shapes=[pltpu.VMEM((1, LANE), F32)],
        compiler_params=pltpu.CompilerParams(dimension_semantics=("arbitrary",), vmem_limit_bytes=VMEM_LIMIT),
        name="inproj",
    )(x2, g, w, b, gq, gk, grp, tri, bound)


def _fox_kernel(first_ref, q_ref, k_ref, v_ref, o_ref, *, tq):
    i = pl.program_id(2)
    step = (pl.program_id(0) * pl.num_programs(1) + pl.program_id(1)) * pl.num_programs(2) + i
    half = tq // 2
    nt = (((1,), (1,)), ((), ()))

    def attend(q, keys, hh, visible=None):
        cols = slice(hh * LANE, (hh + 1) * LANE)
        s = lax.dot_general(q, k_ref[keys, cols], nt, preferred_element_type=F32)
        if visible is not None:
            s = jnp.where(visible, s, NEG_INF)
        return jnp.dot(jnp.exp2(s).astype(BF16), v_ref[keys, cols], preferred_element_type=F32)

    def full_step(j, accs):
        keys = pl.ds(pl.multiple_of(j * tq, tq), tq)
        return tuple(accs[hh] + attend(q_ref[:, hh * LANE:(hh + 1) * LANE], keys, hh) for hh in range(2))

    zero = jnp.zeros((tq, LANE), F32)
    accs = lax.fori_loop(first_ref[step], i, full_step, (zero, zero))

    off = pl.multiple_of(i * tq, tq)
    row_t = lax.broadcasted_iota(jnp.int32, (half, half), 0)
    col_t = lax.broadcasted_iota(jnp.int32, (half, half), 1)
    row_b = lax.broadcasted_iota(jnp.int32, (half, tq), 0) + half
    col_b = lax.broadcasted_iota(jnp.int32, (half, tq), 1)
    lane = lax.broadcasted_iota(jnp.int32, (half, LANE), 1)
    outs = []
    for hh in range(2):
        cols = slice(hh * LANE, (hh + 1) * LANE)
        top = accs[hh][:half] + attend(q_ref[:half, cols], pl.ds(off, half), hh, col_t <= row_t)
        bot = accs[hh][half:] + attend(q_ref[half:, cols], pl.ds(off, tq), hh, col_b <= row_b)
        outs.append((top / top[:, HEAD_DIM:HEAD_DIM + 1], bot / bot[:, HEAD_DIM:HEAD_DIM + 1]))
    for r, rows in enumerate((slice(0, half), slice(half, tq))):
        o_ref[rows, :] = jnp.where(lane < HEAD_DIM, outs[0][r], pltpu.roll(outs[1][r], HEAD_DIM, 1)).astype(BF16)


def _fox(qa, ka, va, c_edges, bsz, t):
    n = qa.shape[0]
    tq = TQ_FOX
    nq = t // tq
    pairs = A_HEADS // 2

    edges = c_edges.reshape(bsz, nq, tq // TM_PROJ, 8, LANE)
    c_first = edges[:, :, 0, 0, :A_HEADS]
    c_last = edges[:, :, -1, 1, :A_HEADS]
    gap = c_first[:, :, None, :] - c_last[:, None, :, :]
    dead = jnp.all((gap < -FOX_SKIP).reshape(bsz, nq, nq, pairs, 2), axis=-1)
    dead = dead & (jnp.arange(nq)[None, :, None, None] > jnp.arange(nq)[None, None, :, None])
    first = jnp.sum(dead, axis=2).astype(jnp.int32).transpose(0, 2, 1).reshape(-1)

    return pl.pallas_call(
        functools.partial(_fox_kernel, tq=tq),
        grid_spec=pltpu.PrefetchScalarGridSpec(
            num_scalar_prefetch=1,
            grid=(bsz, pairs, nq),
            in_specs=[
                pl.BlockSpec((tq, 2 * LANE), lambda b, p, i, first: (b * nq + i, p)),
                pl.BlockSpec((t, 2 * LANE), lambda b, p, i, first: (b, p)),
                pl.BlockSpec((t, 2 * LANE), lambda b, p, i, first: (b, p)),
            ],
            out_specs=pl.BlockSpec((tq, LANE), lambda b, p, i, first: (b * nq + i, p)),
        ),
        out_shape=jax.ShapeDtypeStruct((n, A_DIM), BF16),
        compiler_params=pltpu.CompilerParams(
            dimension_semantics=("arbitrary", "arbitrary", "arbitrary"), vmem_limit_bytes=VMEM_LIMIT),
        name="fox",
    )(first, qa, ka, va)


def _fox_safe_kernel(q_ref, k_ref, v_ref, c_ref, o_ref, *, tq):
    i = pl.program_id(2)
    q = q_ref[...]
    lane = lax.broadcasted_iota(jnp.int32, (tq, LANE), 1)
    zero = jnp.zeros_like(q)
    qh = (jnp.where(lane < HEAD_DIM, q, zero), jnp.where(lane >= HEAD_DIM, q, zero))
    tri = lax.broadcasted_iota(jnp.int32, (tq, tq), 1) <= lax.broadcasted_iota(jnp.int32, (tq, tq), 0)

    def step(j, carry, masked):
        off = pl.multiple_of(j * tq, tq)
        kt = k_ref[pl.ds(off, tq), :]
        vt = v_ref[pl.ds(off, tq), :]
        new = []
        for hh in range(2):
            m, l, acc = carry[hh]
            s = lax.dot_general(qh[hh], kt, (((1,), (1,)), ((), ())), preferred_element_type=F32)
            s = s - c_ref[0, hh:hh + 1, pl.ds(off, tq)]
            if masked:
                s = jnp.where(tri, s, NEG_INF)
            m_new = jnp.maximum(m, jnp.max(s, axis=-1, keepdims=True))
            alpha = jnp.exp2(m - m_new)
            p = jnp.exp2(s - m_new)
            l = alpha * l + jnp.sum(p, axis=-1, keepdims=True)
            acc = alpha * acc + jnp.dot(p.astype(BF16), vt, preferred_element_type=F32)
            new.append((m_new, l, acc))
        return tuple(new)

    init = tuple((jnp.full((tq, 1), NEG_INF, F32), jnp.zeros((tq, 1), F32), jnp.zeros((tq, LANE), F32))
                 for _ in range(2))
    carry = lax.fori_loop(0, i, lambda j, c: step(j, c, False), init)
    (_, l0, a0), (_, l1, a1) = step(i, carry, True)
    o_ref[...] = jnp.where(lane < HEAD_DIM, a0 / l0, a1 / l1).astype(BF16)


def _fox_safe(qa, ka, va, c_edges, bsz, t):
    del c_edges
    n = qa.shape[0]
    tq = TQ_FOX // 2
    nq = t // tq
    pairs = A_HEADS // 2
    heads = lambda z: z.reshape(n, A_HEADS, LANE)
    unpack = lambda z: heads(z)[:, :, :HEAD_DIM].reshape(n, A_DIM)
    c = -jnp.sum(heads(ka)[:, :, HEAD_DIM:HEAD_DIM + 3].astype(F32), axis=-1)
    c_t = c.reshape(bsz, t, A_HEADS).transpose(0, 2, 1).reshape(bsz * pairs, 2, t)
    return pl.pallas_call(
        functools.partial(_fox_safe_kernel, tq=tq),
        grid=(bsz, pairs, nq),
        in_specs=[
            pl.BlockSpec((tq, LANE), lambda b, p, i: (b * nq + i, p)),
            pl.BlockSpec((t, LANE), lambda b, p, i: (b, p)),
            pl.BlockSpec((t, LANE), lambda b, p, i: (b, p)),
            pl.BlockSpec((1, 2, t), lambda b, p, i: (b * pairs + p, 0, 0)),
        ],
        out_specs=pl.BlockSpec((tq, LANE), lambda b, p, i: (b * nq + i, p)),
        out_shape=jax.ShapeDtypeStruct((n, A_DIM), BF16),
        compiler_params=pltpu.CompilerParams(
            dimension_semantics=("arbitrary", "arbitrary", "arbitrary"), vmem_limit_bytes=VMEM_LIMIT),
        name="fox_safe",
    )(unpack(qa), unpack(ka), unpack(va), c_t)


def _mlstm_kernel(qk_ref, halo_ref, v_ref, gates_ref, cw_ref, tri_ref, triu_ref, grp_ref, spread_ref, o_ref,
                  buf_ref, c_st, n_st, m_st, *, tc, chunk):
    j = pl.program_id(1)
    width = B_DIM

    @pl.when(j == 0)
    def _():
        c_st[...] = jnp.zeros_like(c_st)
        n_st[...] = jnp.zeros_like(n_st)
        m_st[...] = jnp.full(m_st.shape, NEG_INF, F32)

    hr = BF16_ROWS
    buf_ref[0:hr, :] = jnp.where(j == 0, 0.0, halo_ref[...].astype(F32))
    buf_ref[hr:hr + tc, :] = qk_ref[...].astype(F32)
    cw = cw_ref[...]
    y = cw[MLSTM_CONV - 1:MLSTM_CONV] * buf_ref[hr:hr + tc, :]
    for d in range(1, MLSTM_CONV):
        y = y + cw[MLSTM_CONV - 1 - d:MLSTM_CONV - d] * buf_ref[hr - d:hr - d + tc, :]
    y = y * _sigmoid(y)
    q_all = y[:, :width]
    k_all = y[:, width:] * (HEAD_DIM ** -0.5)

    head = _head_of_lane((chunk, width))
    head_sq = _head_of_lane((width, width))
    blockdiag = head_sq == lax.broadcasted_iota(jnp.int32, (width, width), 0) // HEAD_DIM
    causal = lax.broadcasted_iota(jnp.int32, (chunk, chunk), 1) <= lax.broadcasted_iota(jnp.int32, (chunk, chunk), 0)
    tri = tri_ref[...]
    grp = grp_ref[...]

    gx = _sum_select(gates_ref[...], spread_ref[...])
    i_all = gx[:, :width]
    f_all = gx[:, width:]

    local = []
    for c in range(tc // chunk):
        r0 = c * chunk
        q = q_all[r0:r0 + chunk]
        k = k_all[r0:r0 + chunk]
        vb = v_ref[r0:r0 + chunk, :]
        qb, kb = q.astype(BF16), k.astype(BF16)
        bc_x = _select_sum(tri, _log_sigmoid(f_all[r0:r0 + chunk]))
        g_t = gates_ref[r0:r0 + chunk, :].T[A_HEADS:A_HEADS + 2 * B_HEADS, :]
        bc_t = _sum_select(_log_sigmoid(g_t), triu_ref[...])

        m_cols, den_cols = [], []
        num_x = jnp.zeros((chunk, width), F32)
        for h in range(B_HEADS):
            lo = h * HEAD_DIM
            row = g_t[h:h + 1, :] - bc_t[B_HEADS + h:B_HEADS + h + 1, :]
            log_d = jnp.where(causal, bc_x[:, lo:lo + 1] + row, NEG_INF)
            m_loc = jnp.max(log_d, axis=-1, keepdims=True)
            s = lax.dot_general(jnp.where(head == h, qb, jnp.zeros_like(qb)), kb, (((1,), (1,)), ((), ())),
                                preferred_element_type=F32)
            sqk = s * jnp.exp(log_d - m_loc)
            num_x = jnp.where(head == h, jnp.dot(sqk.astype(BF16), vb, preferred_element_type=F32), num_x)
            m_cols.append(m_loc)
            den_cols.append(jnp.sum(sqk, axis=-1, keepdims=True))

        g_x = bc_x[chunk - 1:chunk, :]
        a_x = g_x - bc_x + i_all[r0:r0 + chunk]
        a_max = jnp.max(a_x, axis=0, keepdims=True)
        kw = k * jnp.exp(a_x - a_max)
        upd = lax.dot_general(kw.astype(BF16), vb, (((0,), (0,)), ((), ())), preferred_element_type=F32)
        local.append((q, qb, bc_x, _expand_heads(m_cols, head), num_x, _expand_heads(den_cols, head), g_x, a_max,
                      jnp.where(blockdiag, upd, 0.0), jnp.sum(kw, axis=0, keepdims=True)))

    for c, (q, qb, bc_x, m_loc_x, num_x, den_x, g_x, a_max, upd, k_sum) in enumerate(local):
        r0 = c * chunk
        c_prev, n_prev, m_prev = c_st[...], n_st[...], m_st[...]
        m_inter = bc_x + m_prev
        m_out_x = jnp.maximum(m_inter, m_loc_x)
        intra_w = jnp.exp(m_loc_x - m_out_x)
        inter_w = jnp.exp(m_inter - m_out_x)
        q_c = jnp.dot(qb, c_prev.astype(BF16), preferred_element_type=F32)
        q_n = _sum_select(q * n_prev, grp, parts=2)
        num = intra_w * num_x + inter_w * q_c
        den = intra_w * den_x + inter_w * q_n
        o_ref[r0:r0 + chunk, :] = (num / jnp.maximum(jnp.abs(den), jnp.exp(-m_out_x))).astype(BF16)

        m_new = jnp.maximum(g_x + m_prev, a_max)
        decay = jnp.exp(g_x + m_prev - m_new)
        fresh = jnp.exp(a_max - m_new)
        c_st[...] = decay * c_prev + fresh * upd
        n_st[...] = decay * n_prev + fresh * k_sum
        m_st[...] = m_new


def _mlstm(rest, gates, conv_w, tri_l, tri_u, grp, spread, bsz, t):
    n = rest.shape[0]
    tc, nt = TC_MLSTM, t // TC_MLSTM
    hr = BF16_ROWS
    const = lambda b, j: (0, 0)
    return pl.pallas_call(
        functools.partial(_mlstm_kernel, tc=tc, chunk=L_MLSTM),
        grid=(bsz, nt),
        in_specs=[
            pl.BlockSpec((tc, 2 * B_DIM), lambda b, j: (b * nt + j, 0)),
            pl.BlockSpec((hr, 2 * B_DIM), lambda b, j: (jnp.maximum((b * nt + j) * (tc // hr) - 1, 0), 0)),
            pl.BlockSpec((tc, B_DIM), lambda b, j: (b * nt + j, 2)),
            pl.BlockSpec((tc, LANE), lambda b, j: (b * nt + j, 0)),
            pl.BlockSpec((MLSTM_CONV, 2 * B_DIM), const),
            pl.BlockSpec((L_MLSTM, L_MLSTM), const),
            pl.BlockSpec((L_MLSTM, L_MLSTM), const),
            pl.BlockSpec((B_DIM, B_DIM), const),
            pl.BlockSpec((LANE, 2 * B_DIM), const),
        ],
        out_specs=pl.BlockSpec((tc, B_DIM), lambda b, j: (b * nt + j, 0)),
        out_shape=jax.ShapeDtypeStruct((n, B_DIM), BF16),
        scratch_shapes=[
            pltpu.VMEM((tc + hr, 2 * B_DIM), F32),
            pltpu.VMEM((B_DIM, B_DIM), F32),
            pltpu.VMEM((1, B_DIM), F32),
            pltpu.VMEM((1, B_DIM), F32),
        ],
        compiler_params=pltpu.CompilerParams(dimension_semantics=("arbitrary", "arbitrary"),
                                             vmem_limit_bytes=VMEM_LIMIT),
        name="mlstm",
    )(rest, rest, rest, gates, conv_w, tri_l, tri_u, grp, spread)


def _hgrn_kernel(q_ref, f_ref, i_ref, la_ref, l1m_ref, tri_ref, tot_ref, o_ref, st_ref, *, tc, chunk):
    nb = q_ref.shape[0]
    width = C_DIM

    @pl.when(pl.program_id(0) == 0)
    def _():
        st_ref[...] = jnp.zeros_like(st_ref)

    head = _head_of_lane((tc, width))
    head_sq = _head_of_lane((width, width))
    blockdiag = head_sq == lax.broadcasted_iota(jnp.int32, (width, width), 0) // HEAD_DIM
    rr = lax.broadcasted_iota(jnp.int32, (tc, tc), 0)
    cc = lax.broadcasted_iota(jnp.int32, (tc, tc), 1)
    keep = (cc <= rr) & (cc // chunk == rr // chunk)

    seqs = []
    for s in range(nb):
        xq = q_ref[s].astype(F32)
        q = xq * _sigmoid(xq)
        xf = f_ref[s].astype(F32)
        ls = _log_sigmoid(xf)
        a = la_ref[...]
        b = l1m_ref[...] + ls
        logf = jnp.maximum(a, b) + jnp.log(1.0 + jnp.exp(-jnp.abs(a - b)))
        kk = jnp.exp(l1m_ref[...] + ls - xf)
        vb = i_ref[s]

        bc = _select_sum(tri_ref[...], logf)
        blast = _select_sum(tot_ref[...], logf)
        qt = (q * jnp.exp(bc)).astype(BF16)
        kt = (kk * jnp.exp(-bc)).astype(BF16)
        kd = (kk * jnp.exp(blast - bc)).astype(BF16)

        intra = jnp.zeros((tc, width), F32)
        for h in range(C_HEADS):
            att = lax.dot_general(jnp.where(head == h, qt, jnp.zeros_like(qt)), kt, (((1,), (1,)), ((), ())),
                                  preferred_element_type=F32)
            att = jnp.where(keep, att, 0.0).astype(BF16)
            intra = jnp.where(head == h, jnp.dot(att, vb, preferred_element_type=F32), intra)
        seqs.append((qt, kd, vb, jnp.exp(blast), intra))

    for c in range(tc // chunk):
        r0 = c * chunk
        for s, (qt, kd, vb, chunk_decay, intra) in enumerate(seqs):
            s_t = st_ref[s]
            inter = lax.dot_general(qt[r0:r0 + chunk], s_t.astype(BF16), (((1,), (1,)), ((), ())),
                                    preferred_element_type=F32)
            o_ref[s, r0:r0 + chunk, :] = (intra[r0:r0 + chunk] + inter).astype(BF16)
            upd = lax.dot_general(vb[r0:r0 + chunk], kd[r0:r0 + chunk], (((0,), (0,)), ((), ())),
                                  preferred_element_type=F32)
            st_ref[s] = s_t * chunk_decay[r0:r0 + 1, :] + jnp.where(blockdiag, upd, 0.0)


def _hgrn(rest, log_lb, log_1m_lb, tri_bd, tot_bd, bsz, t):
    tc, nt = TC_HGRN, t // TC_HGRN
    rest3 = rest.reshape(bsz, t, REST_W)
    blk = lambda col: pl.BlockSpec((bsz, tc, C_DIM), lambda j: (0, j, col))
    const = lambda j: (0, 0)
    out = pl.pallas_call(
        functools.partial(_hgrn_kernel, tc=tc, chunk=L_HGRN),
        grid=(nt,),
        in_specs=[
            blk(4), blk(5), blk(6),
            pl.BlockSpec((1, C_DIM), const),
            pl.BlockSpec((1, C_DIM), const),
            pl.BlockSpec((tc, tc), const),
            pl.BlockSpec((tc, tc), const),
        ],
        out_specs=pl.BlockSpec((bsz, tc, C_DIM), lambda j: (0, j, 0)),
        out_shape=jax.ShapeDtypeStruct((bsz, t, C_DIM), BF16),
        scratch_shapes=[pltpu.VMEM((bsz, C_DIM, C_DIM), F32)],
        compiler_params=pltpu.CompilerParams(dimension_semantics=("arbitrary",), vmem_limit_bytes=VMEM_LIMIT),
        name="hgrn2",
    )(rest3, rest3, rest3, log_lb, log_1m_lb, tri_bd, tot_bd)
    return out.reshape(bsz * t, C_DIM)


def _outproj_kernel(x_ref, oa_ref, ob_ref, oc_ref, bo_ref, cg_ref, g_ref, grp_ref, w_ref, o_ref):
    def normed(o, lo, hi):
        return o * lax.rsqrt(_group_mean_sq(o, grp_ref) + EPS) * g_ref[:, lo:hi]

    ya = normed(oa_ref[...].astype(F32), 0, A_DIM)
    yb = _sigmoid(bo_ref[...].astype(F32)) * normed(ob_ref[...].astype(F32), A_DIM, A_DIM + B_DIM)
    cg = cg_ref[...].astype(F32)
    yc = cg * _sigmoid(cg) * normed(oc_ref[...].astype(F32), A_DIM + B_DIM, D_MIX)
    acc = x_ref[...] + jnp.dot(ya.astype(BF16), w_ref[0:A_DIM, :], preferred_element_type=F32)
    acc = acc + jnp.dot(yb.astype(BF16), w_ref[A_DIM:A_DIM + B_DIM, :], preferred_element_type=F32)
    acc = acc + jnp.dot(yc.astype(BF16), w_ref[A_DIM + B_DIM:D_MIX, :], preferred_element_type=F32)
    o_ref[...] = acc


def _outproj(x2, oa, ob, oc, rest, out_g, grp, w_out):
    n = x2.shape[0]
    tm = TM_PROJ
    const = lambda i: (0, 0)
    return pl.pallas_call(
        _outproj_kernel,
        grid=(n // tm,),
        in_specs=[
            pl.BlockSpec((tm, D_MODEL), lambda i: (i, 0)),
            pl.BlockSpec((tm, A_DIM), lambda i: (i, 0)),
            pl.BlockSpec((tm, B_DIM), lambda i: (i, 0)),
            pl.BlockSpec((tm, C_DIM), lambda i: (i, 0)),
            pl.BlockSpec((tm, B_DIM), lambda i: (i, 3)),
            pl.BlockSpec((tm, C_DIM), lambda i: (i, 7)),
            pl.BlockSpec((1, D_MIX), const),
            pl.BlockSpec((A_DIM, A_DIM), const),
            pl.BlockSpec((D_MIX, D_MODEL), const),
        ],
        out_specs=pl.BlockSpec((tm, D_MODEL), lambda i: (i, 0)),
        out_shape=jax.ShapeDtypeStruct((n, D_MODEL), F32),
        compiler_params=pltpu.CompilerParams(dimension_semantics=("arbitrary",), vmem_limit_bytes=VMEM_LIMIT),
        name="outproj",
    )(x2, oa, ob, oc, rest, rest, out_g, grp, w_out)


def _ffn_kernel(x_ref, halo_ref, g_ref, wup_ref, cw_ref, cb_ref, wdn_ref, o_ref, h_ref, ug_ref, uu_ref, act_ref,
                *, tm, tiles_per_seq):
    i = pl.program_id(0)
    halo_rows = BF16_ROWS

    def rms(x):
        ms = jnp.mean(x * x, axis=-1, keepdims=True)
        return x * lax.rsqrt(ms + EPS) * g_ref[...]

    x = x_ref[...]
    first = (i % tiles_per_seq) == 0
    h_ref[0:halo_rows, :] = jnp.where(first, 0.0, rms(halo_ref[...])).astype(BF16)
    h_ref[halo_rows:, :] = rms(x).astype(BF16)
    h = h_ref[...]

    def conv(u_ref, slot, lo):
        y = cb_ref[:, lo:lo + FF_CHUNK] + cw_ref[FFN_CONV - 1:FFN_CONV, lo:lo + FF_CHUNK] * u_ref[slot, halo_rows:, :]
        for d in range(1, FFN_CONV):
            y = y + cw_ref[FFN_CONV - 1 - d:FFN_CONV - d, lo:lo + FF_CHUNK] * u_ref[slot, pl.ds(halo_rows - d, tm), :]
        return y

    def up_proj(c):
        lo = c * FF_CHUNK
        ug_ref[c % 2] = jnp.dot(h, wup_ref[:, lo:lo + FF_CHUNK], preferred_element_type=F32)
        uu_ref[c % 2] = jnp.dot(h, wup_ref[:, D_FF + lo:D_FF + lo + FF_CHUNK], preferred_element_type=F32)

    n_chunks = D_FF // FF_CHUNK
    up_proj(0)
    for c in range(n_chunks):
        if c + 1 < n_chunks:
            up_proj(c + 1)
        lo = c * FF_CHUNK
        gate = conv(ug_ref, c % 2, lo)
        up = conv(uu_ref, c % 2, D_FF + lo)
        act_ref[:, lo:lo + FF_CHUNK] = (gate * _sigmoid(gate) * up).astype(BF16)
    o_ref[...] = x + jnp.dot(act_ref[...], wdn_ref[...], preferred_element_type=F32)


def _ffn(x2, g, w_up, conv_w, conv_b, w_down, t):
    n = x2.shape[0]
    tm = TM_PROJ
    halo = BF16_ROWS
    const = lambda i: (0, 0)
    return pl.pallas_call(
        functools.partial(_ffn_kernel, tm=tm, tiles_per_seq=t // tm),
        grid=(n // tm,),
        in_specs=[
            pl.BlockSpec((tm, D_MODEL), lambda i: (i, 0)),
            pl.BlockSpec((halo, D_MODEL), lambda i: (jnp.maximum(i * (tm // halo) - 1, 0), 0)),
            pl.BlockSpec((1, D_MODEL), const),
            pl.BlockSpec((D_MODEL, 2 * D_FF), const),
            pl.BlockSpec((FFN_CONV, 2 * D_FF), const),
            pl.BlockSpec((1, 2 * D_FF), const),
            pl.BlockSpec((D_FF, D_MODEL), const),
        ],
        out_specs=pl.BlockSpec((tm, D_MODEL), lambda i: (i, 0)),
        out_shape=jax.ShapeDtypeStruct((n, D_MODEL), F32),
        scratch_shapes=[
            pltpu.VMEM((tm + halo, D_MODEL), BF16),
            pltpu.VMEM((2, tm + halo, FF_CHUNK), F32),
            pltpu.VMEM((2, tm + halo, FF_CHUNK), F32),
            pltpu.VMEM((tm, D_FF), BF16),
        ],
        compiler_params=pltpu.CompilerParams(dimension_semantics=("arbitrary",), vmem_limit_bytes=VMEM_LIMIT),
        name="ffn",
    )(x2, x2, g, w_up, conv_w, conv_b, w_down)


def _block_diag_ones(n, blk, dtype):
    r = np.arange(n) // blk
    return jnp.asarray(r[:, None] == r[None, :], dtype)


def kernel(x, lb_logits, norm_mix_g, w_in, b_in, a_q_g, a_k_g, b_conv_w, out_g, w_out, norm_ffn_g, w_up,
           ffn_conv_w, ffn_conv_b, w_down):
    bsz, t, d = x.shape
    depth = w_in.shape[0]
    assert d == D_MODEL and t % TQ_FOX == 0 and t % TC_MLSTM == 0 and t % TC_HGRN == 0
    n = bsz * t
    x2 = x.reshape(n, d).astype(F32)

    p = jax.nn.softmax(lb_logits.astype(F32), axis=0)
    lb_all = jnp.maximum(jnp.cumsum(p, axis=0) - p[0], 0.0)
    log_lb = jnp.log(lb_all)
    log_1m_lb = jnp.log1p(-lb_all)

    grp_a = _block_diag_ones(A_DIM, HEAD_DIM, BF16)
    grp_b = _block_diag_ones(B_DIM, HEAD_DIM, BF16)
    ar = np.arange(L_MLSTM)
    tri_u = jnp.asarray(ar[:, None] <= ar[None, :], BF16)
    tri_l = jnp.asarray(ar[:, None] >= ar[None, :], BF16)
    gate_row = np.arange(LANE)[:, None] - A_HEADS
    spread = jnp.asarray(gate_row == np.arange(2 * B_DIM)[None, :] // HEAD_DIM, BF16)
    ap = np.arange(TM_PROJ)
    tri_proj = jnp.asarray(ap[:, None] >= ap[None, :], BF16)
    ah = np.arange(TC_HGRN)
    same = (ah[:, None] // L_HGRN) == (ah[None, :] // L_HGRN)
    tri_bd = jnp.asarray(same & (ah[:, None] >= ah[None, :]), BF16)
    tot_bd = jnp.asarray(same, BF16)

    for l in range(depth):
        w_r = _reorder_in_cols(w_in[l].astype(BF16))
        b_r = _reorder_in_cols(b_in[l].astype(F32))[None, :]
        gq = (jnp.tile(a_q_g[l].astype(F32), A_HEADS) * (HEAD_DIM ** -0.5 * LOG2E))[None, :]
        gk = jnp.tile(a_k_g[l].astype(F32), A_HEADS)[None, :]
        bound = HEAD_DIM * jnp.max(jnp.abs(gq)) * jnp.max(jnp.abs(gk))
        qa, ka, va, rest, gates, c_edges = _inproj(x2, norm_mix_g[l][None, :].astype(F32), w_r, b_r, gq, gk, grp_a, tri_proj,
                                          jnp.full((1, LANE), bound, F32), t)
        oa = lax.cond(bound < FOX_BOUND_MAX, functools.partial(_fox, bsz=bsz, t=t),
                      functools.partial(_fox_safe, bsz=bsz, t=t), qa, ka, va, c_edges)
        ob = _mlstm(rest, gates, b_conv_w[l].astype(F32), tri_l, tri_u, grp_b, spread, bsz, t)
        oc = _hgrn(rest, log_lb[l][None, :], log_1m_lb[l][None, :], tri_bd, tot_bd, bsz, t)
        x2 = _outproj(x2, oa, ob, oc, rest, out_g[l][None, :].astype(F32), grp_a, w_out[l].astype(BF16))
        x2 = _ffn(x2, norm_ffn_g[l][None, :].astype(F32), w_up[l].astype(BF16), ffn_conv_w[l].astype(F32),
                  ffn_conv_b[l][None, :].astype(F32), w_down[l].astype(BF16), t)
    return x2.reshape(bsz, t, d).astype(x.dtype)
```

```python
import functools

import numpy as np
import jax
import jax.numpy as jnp
from jax import lax
from jax.experimental import pallas as pl
from jax.experimental.pallas import tpu as pltpu

F32 = jnp.float32
BF16 = jnp.bfloat16
NEG_INF = float("-inf")
LOG2E = 1.4426950408889634

D_MODEL = 1024
HEAD_DIM = 64
A_HEADS, B_HEADS, C_HEADS = 8, 4, 4
A_DIM, B_DIM, C_DIM = A_HEADS * HEAD_DIM, B_HEADS * HEAD_DIM, C_HEADS * HEAD_DIM
D_MIX = A_DIM + B_DIM + C_DIM
D_FF = 2816
MLSTM_CONV = 4
FFN_CONV = 3
EPS = 1e-6

LANE = 128
BF16_ROWS = 16
VMEM_LIMIT = 56 * 1024 * 1024

QKV_W = 3 * A_DIM
REST_W = 8 * B_DIM
IN_W = QKV_W + REST_W + LANE

TM_PROJ = 512
TM_OUT = 1024
TQ_FOX = 1024
FOX_DIAG_BANDS = 2
FOX_BOUND_MAX = 40.0
FOX_SKIP = 160.0
TC_MLSTM = 512
L_MLSTM = 128
TC_HGRN = 256
L_HGRN = 16
FF_CHUNK = 256


def _reorder_in_cols(w):
    a_f = 3 * A_DIM
    b_qk = a_f + A_HEADS
    b_v = b_qk + 2 * B_DIM
    b_i = b_v + B_DIM
    b_f = b_i + B_HEADS
    b_o = b_f + B_HEADS
    c_q = b_o + B_DIM
    end = c_q + 4 * C_DIM
    sl = lambda lo, hi: lax.slice_in_dim(w, lo, hi, axis=-1)
    n_gate = A_HEADS + 2 * B_HEADS
    pad = jnp.zeros(w.shape[:-1] + (LANE - n_gate,), w.dtype)
    out = jnp.concatenate([sl(0, a_f), sl(b_qk, b_i), sl(b_o, end), sl(a_f, b_qk), sl(b_i, b_o), pad], axis=-1)
    assert out.shape[-1] == IN_W
    return out


def _log_sigmoid(x):
    return jnp.minimum(x, 0.0) - jnp.log(1.0 + jnp.exp(-jnp.abs(x)))


def _sigmoid(x):
    return 1.0 / (1.0 + jnp.exp(-x))


def _head_of_lane(shape):
    return lax.broadcasted_iota(jnp.int32, shape, len(shape) - 1) // HEAD_DIM


def _expand_heads(cols, head):
    out = cols[-1]
    for h in range(len(cols) - 2, -1, -1):
        out = jnp.where(head == h, cols[h], out)
    return out


def _group_mean_sq(z, grp_ref):
    w = z.shape[-1]
    zz = (z * z).astype(BF16)
    return jnp.dot(zz, grp_ref[0:w, 0:w], preferred_element_type=F32) * (1.0 / HEAD_DIM)


def _split3(x):
    hi = x.astype(BF16).astype(F32)
    mid = (x - hi).astype(BF16).astype(F32)
    lo = (x - hi - mid).astype(BF16).astype(F32)
    return hi, mid, lo


def _select_sum(sel, x, parts=3):
    out = None
    for term in _split3(x)[:parts]:
        d = jnp.dot(sel, term.astype(BF16), preferred_element_type=F32)
        out = d if out is None else out + d
    return out


def _sum_select(x, sel, parts=3):
    out = None
    for term in _split3(x)[:parts]:
        d = jnp.dot(term.astype(BF16), sel, preferred_element_type=F32)
        out = d if out is None else out + d
    return out


def _inproj_kernel(x_ref, g_ref, w_ref, b_ref, gq_ref, gk_ref, grp_ref, tri_ref, bound_ref,
                   qa_ref, ka_ref, va_ref, rest_ref, gates_ref, edges_ref, carry_ref, *, tiles_per_seq):
    i = pl.program_id(0)
    tm = x_ref.shape[0]

    @pl.when(i % tiles_per_seq == 0)
    def _():
        carry_ref[...] = jnp.zeros_like(carry_ref)

    x = x_ref[...]
    ms = jnp.mean(x * x, axis=-1, keepdims=True)
    h = (x * lax.rsqrt(ms + EPS) * g_ref[...]).astype(BF16)

    def proj(lo, hi):
        return jnp.dot(h, w_ref[:, lo:hi], preferred_element_type=F32) + b_ref[:, lo:hi]

    gates = proj(QKV_W + REST_W, IN_W)
    gates_ref[...] = gates
    q = proj(0, A_DIM)
    k = proj(A_DIM, 2 * A_DIM)
    v = proj(2 * A_DIM, QKV_W)

    cs = _select_sum(tri_ref[...], _log_sigmoid(gates))
    c2 = (cs + carry_ref[...]) * LOG2E
    carry_ref[...] = carry_ref[...] + cs[tm - 1:tm, :]
    edge_row = lax.broadcasted_iota(jnp.int32, (8, LANE), 0)
    edges_ref[0] = jnp.where(edge_row == 0, c2[0:1, :], jnp.where(edge_row == 1, c2[tm - 1:tm, :], 0.0))

    q = q * lax.rsqrt(_group_mean_sq(q, grp_ref) + EPS) * gq_ref[...]
    k = k * lax.rsqrt(_group_mean_sq(k, grp_ref) + EPS) * gk_ref[...]
    for c in range(REST_W // A_DIM):
        rest_ref[:, c * A_DIM:(c + 1) * A_DIM] = proj(QKV_W + c * A_DIM, QKV_W + (c + 1) * A_DIM).astype(BF16)

    lane = lax.broadcasted_iota(jnp.int32, (tm, LANE), 1)
    lo_slot = (lane >= HEAD_DIM) & (lane < HEAD_DIM + 3)
    hi_slot = (lane >= HEAD_DIM + 3) & (lane < HEAD_DIM + 6)
    last = lane == HEAD_DIM + 6
    base_q = jnp.where(lo_slot, -1.0, jnp.where(last, 1.0, 0.0))
    base_k = jnp.where(hi_slot, 1.0, jnp.where(last, -bound_ref[...], 0.0))
    ones_v = jnp.where(lane == HEAD_DIM, 1.0, 0.0)
    data = lane < HEAD_DIM
    term = lane % 3
    assert HEAD_DIM % 3 == 1
    for hd in range(A_HEADS):
        pair = slice((hd // 2) * LANE, (hd // 2 + 1) * LANE)
        out = slice(hd * LANE, (hd + 1) * LANE)
        place = (lambda z: z) if hd % 2 == 0 else (lambda z: pltpu.roll(z, HEAD_DIM, 1))
        c_hi, c_mid, c_lo = _split3(jnp.broadcast_to(c2[:, hd:hd + 1], (tm, LANE)))
        terms = jnp.where(term == 1, c_hi, jnp.where(term == 2, c_mid, c_lo))
        aug_q = jnp.where(hi_slot, terms, base_q)
        aug_k = jnp.where(lo_slot, terms, base_k)
        qa_ref[:, out] = jnp.where(data, place(q[:, pair]), aug_q).astype(BF16)
        ka_ref[:, out] = jnp.where(data, place(k[:, pair]), aug_k).astype(BF16)
        va_ref[:, out] = jnp.where(data, place(v[:, pair]), ones_v).astype(BF16)


def _inproj(x2, g, w, b, gq, gk, grp, tri, bound, t):
    n = x2.shape[0]
    tm = TM_PROJ
    aug_w = A_HEADS * LANE
    const = lambda i: (0, 0)
    row = lambda width: pl.BlockSpec((tm, width), lambda i: (i, 0))
    return pl.pallas_call(
        functools.partial(_inproj_kernel, tiles_per_seq=t // tm),
        grid=(n // tm,),
        in_specs=[
            row(D_MODEL),
            pl.BlockSpec((1, D_MODEL), const),
            pl.BlockSpec((D_MODEL, IN_W), const),
            pl.BlockSpec((1, IN_W), const),
            pl.BlockSpec((1, A_DIM), const),
            pl.BlockSpec((1, A_DIM), const),
            pl.BlockSpec((A_DIM, A_DIM), const),
            pl.BlockSpec((tm, tm), const),
            pl.BlockSpec((1, LANE), const),
        ],
        out_specs=[row(aug_w), row(aug_w), row(aug_w), row(REST_W), row(LANE),
                   pl.BlockSpec((1, 8, LANE), lambda i: (i, 0, 0))],
        out_shape=[
            jax.ShapeDtypeStruct((n, aug_w), BF16),
            jax.ShapeDtypeStruct((n, aug_w), BF16),
            jax.ShapeDtypeStruct((n, aug_w), BF16),
            jax.ShapeDtypeStruct((n, REST_W), BF16),
            jax.ShapeDtypeStruct((n, LANE), F32),
            jax.ShapeDtypeStruct((n // tm, 8, LANE), F32),
        ],
        scratch_shapes=[pltpu.VMEM((1, LANE), F32)],
        compiler_params=pltpu.CompilerParams(dimension_semantics=("arbitrary",), vmem_limit_bytes=VMEM_LIMIT),
        name="inproj",
    )(x2, g, w, b, gq, gk, grp, tri, bound)


def _fox_kernel(first_ref, q_ref, k_ref, v_ref, o_ref, *, tq):
    i = pl.program_id(2)
    step = (pl.program_id(0) * pl.num_programs(1) + pl.program_id(1)) * pl.num_programs(2) + i
    nt = (((1,), (1,)), ((), ()))
    half = tq // 2

    def attend(q, keys, hh, visible=None):
        cols = slice(hh * LANE, (hh + 1) * LANE)
        s = lax.dot_general(q, k_ref[keys, cols], nt, preferred_element_type=F32)
        if visible is not None:
            s = jnp.where(visible, s, NEG_INF)
        return jnp.dot(jnp.exp2(s).astype(BF16), v_ref[keys, cols], preferred_element_type=F32)

    def full_step(j, accs):
        keys = pl.ds(pl.multiple_of(j * tq, tq), tq)
        return tuple(accs[hh] + attend(q_ref[:, hh * LANE:(hh + 1) * LANE], keys, hh) for hh in range(2))

    dead_halves = first_ref[step]

    def half_step():
        keys = pl.ds(pl.multiple_of(dead_halves * half, half), half)
        return tuple(attend(q_ref[:, hh * LANE:(hh + 1) * LANE], keys, hh) for hh in range(2))

    zero = jnp.zeros((tq, LANE), F32)
    accs = lax.cond(dead_halves % 2 == 1, half_step, lambda: (zero, zero))
    accs = lax.fori_loop((dead_halves + 1) // 2, i, full_step, accs)

    off = pl.multiple_of(i * tq, tq)
    band = tq // FOX_DIAG_BANDS
    lane = lax.broadcasted_iota(jnp.int32, (band, LANE), 1)
    for r in range(FOX_DIAG_BANDS):
        rows = slice(r * band, (r + 1) * band)
        n_keys = (r + 1) * band
        visible = (lax.broadcasted_iota(jnp.int32, (band, n_keys), 1)
                   <= lax.broadcasted_iota(jnp.int32, (band, n_keys), 0) + r * band)
        outs = []
        for hh in range(2):
            cols = slice(hh * LANE, (hh + 1) * LANE)
            acc = accs[hh][rows] + attend(q_ref[rows, cols], pl.ds(off, n_keys), hh, visible)
            outs.append(acc / acc[:, HEAD_DIM:HEAD_DIM + 1])
        o_ref[rows, :] = jnp.where(lane < HEAD_DIM, outs[0], pltpu.roll(outs[1], HEAD_DIM, 1)).astype(BF16)


def _fox(qa, ka, va, c_edges, bsz, t):
    n = qa.shape[0]
    tq = TQ_FOX
    nq = t // tq
    pairs = A_HEADS // 2

    assert tq == 2 * TM_PROJ
    edges = c_edges.reshape(bsz, 2 * nq, 8, LANE)
    c_first = edges[:, 0::2, 0, :A_HEADS]
    c_last = edges[:, :, 1, :A_HEADS]
    gap = c_first[:, :, None, :] - c_last[:, None, :, :]
    dead = jnp.all((gap < -FOX_SKIP).reshape(bsz, nq, 2 * nq, pairs, 2), axis=-1)
    dead = dead & (2 * jnp.arange(nq)[None, :, None, None] > jnp.arange(2 * nq)[None, None, :, None])
    first = jnp.sum(dead, axis=2).astype(jnp.int32).transpose(0, 2, 1).reshape(-1)

    return pl.pallas_call(
        functools.partial(_fox_kernel, tq=tq),
        grid_spec=pltpu.PrefetchScalarGridSpec(
            num_scalar_prefetch=1,
            grid=(bsz, pairs, nq),
            in_specs=[
                pl.BlockSpec((tq, 2 * LANE), lambda b, p, i, first: (b * nq + i, p)),
                pl.BlockSpec((t, 2 * LANE), lambda b, p, i, first: (b, p)),
                pl.BlockSpec((t, 2 * LANE), lambda b, p, i, first: (b, p)),
            ],
            out_specs=pl.BlockSpec((tq, LANE), lambda b, p, i, first: (b * nq + i, p)),
        ),
        out_shape=jax.ShapeDtypeStruct((n, A_DIM), BF16),
        compiler_params=pltpu.CompilerParams(
            dimension_semantics=("arbitrary", "arbitrary", "arbitrary"), vmem_limit_bytes=VMEM_LIMIT),
        name="fox",
    )(first, qa, ka, va)


def _fox_safe_kernel(q_ref, k_ref, v_ref, c_ref, o_ref, *, tq):
    i = pl.program_id(2)
    q = q_ref[...]
    lane = lax.broadcasted_iota(jnp.int32, (tq, LANE), 1)
    zero = jnp.zeros_like(q)
    qh = (jnp.where(lane < HEAD_DIM, q, zero), jnp.where(lane >= HEAD_DIM, q, zero))
    tri = lax.broadcasted_iota(jnp.int32, (tq, tq), 1) <= lax.broadcasted_iota(jnp.int32, (tq, tq), 0)

    def step(j, carry, masked):
        off = pl.multiple_of(j * tq, tq)
        kt = k_ref[pl.ds(off, tq), :]
        vt = v_ref[pl.ds(off, tq), :]
        new = []
        for hh in range(2):
            m, l, acc = carry[hh]
            s = lax.dot_general(qh[hh], kt, (((1,), (1,)), ((), ())), preferred_element_type=F32)
            s = s - c_ref[0, hh:hh + 1, pl.ds(off, tq)]
            if masked:
                s = jnp.where(tri, s, NEG_INF)
            m_new = jnp.maximum(m, jnp.max(s, axis=-1, keepdims=True))
            alpha = jnp.exp2(m - m_new)
            p = jnp.exp2(s - m_new)
            l = alpha * l + jnp.sum(p, axis=-1, keepdims=True)
            acc = alpha * acc + jnp.dot(p.astype(BF16), vt, preferred_element_type=F32)
            new.append((m_new, l, acc))
        return tuple(new)

    init = tuple((jnp.full((tq, 1), NEG_INF, F32), jnp.zeros((tq, 1), F32), jnp.zeros((tq, LANE), F32))
                 for _ in range(2))
    carry = lax.fori_loop(0, i, lambda j, c: step(j, c, False), init)
    (_, l0, a0), (_, l1, a1) = step(i, carry, True)
    o_ref[...] = jnp.where(lane < HEAD_DIM, a0 / l0, a1 / l1).astype(BF16)


def _fox_safe(qa, ka, va, c_edges, bsz, t):
    del c_edges
    n = qa.shape[0]
    tq = TQ_FOX // 2
    nq = t // tq
    pairs = A_HEADS // 2
    heads = lambda z: z.reshape(n, A_HEADS, LANE)
    unpack = lambda z: heads(z)[:, :, :HEAD_DIM].reshape(n, A_DIM)
    c = jnp.sum(heads(ka)[:, :, HEAD_DIM:HEAD_DIM + 3].astype(F32), axis=-1)
    c_t = c.reshape(bsz, t, A_HEADS).transpose(0, 2, 1).reshape(bsz * pairs, 2, t)
    return pl.pallas_call(
        functools.partial(_fox_safe_kernel, tq=tq),
        grid=(bsz, pairs, nq),
        in_specs=[
            pl.BlockSpec((tq, LANE), lambda b, p, i: (b * nq + i, p)),
            pl.BlockSpec((t, LANE), lambda b, p, i: (b, p)),
            pl.BlockSpec((t, LANE), lambda b, p, i: (b, p)),
            pl.BlockSpec((1, 2, t), lambda b, p, i: (b * pairs + p, 0, 0)),
        ],
        out_specs=pl.BlockSpec((tq, LANE), lambda b, p, i: (b * nq + i, p)),
        out_shape=jax.ShapeDtypeStruct((n, A_DIM), BF16),
        compiler_params=pltpu.CompilerParams(
            dimension_semantics=("arbitrary", "arbitrary", "arbitrary"), vmem_limit_bytes=VMEM_LIMIT),
        name="fox_safe",
    )(unpack(qa), unpack(ka), unpack(va), c_t)


def _mlstm_kernel(qk_ref, halo_ref, v_ref, gates_ref, cw_ref, tri_ref, triu_ref, grp_ref, spread_ref, o_ref,
                  buf_ref, c_st, n_st, m_st, *, tc, chunk):
    j = pl.program_id(1)
    width = B_DIM

    @pl.when(j == 0)
    def _():
        c_st[...] = jnp.zeros_like(c_st)
        n_st[...] = jnp.zeros_like(n_st)
        m_st[...] = jnp.full(m_st.shape, NEG_INF, F32)

    hr = BF16_ROWS
    buf_ref[0:hr, :] = jnp.where(j == 0, 0.0, halo_ref[...].astype(F32))
    buf_ref[hr:hr + tc, :] = qk_ref[...].astype(F32)
    cw = cw_ref[...]
    y = cw[MLSTM_CONV - 1:MLSTM_CONV] * buf_ref[hr:hr + tc, :]
    for d in range(1, MLSTM_CONV):
        y = y + cw[MLSTM_CONV - 1 - d:MLSTM_CONV - d] * buf_ref[hr - d:hr - d + tc, :]
    y = y * _sigmoid(y)
    q_all = y[:, :width]
    k_all = y[:, width:] * (HEAD_DIM ** -0.5)

    head = _head_of_lane((chunk, width))
    head_sq = _head_of_lane((width, width))
    blockdiag = head_sq == lax.broadcasted_iota(jnp.int32, (width, width), 0) // HEAD_DIM
    causal = lax.broadcasted_iota(jnp.int32, (chunk, chunk), 1) <= lax.broadcasted_iota(jnp.int32, (chunk, chunk), 0)
    tri = tri_ref[...]
    grp = grp_ref[...]

    gx = _sum_select(gates_ref[...], spread_ref[...])
    i_all = gx[:, :width]
    f_all = gx[:, width:]

    local = []
    for c in range(tc // chunk):
        r0 = c * chunk
        q = q_all[r0:r0 + chunk]
        k = k_all[r0:r0 + chunk]
        vb = v_ref[r0:r0 + chunk, :]
        qb, kb = q.astype(BF16), k.astype(BF16)
        bc_x = _select_sum(tri, _log_sigmoid(f_all[r0:r0 + chunk]))
        g_t = gates_ref[r0:r0 + chunk, :].T[A_HEADS:A_HEADS + 2 * B_HEADS, :]
        bc_t = _sum_select(_log_sigmoid(g_t), triu_ref[...])

        m_cols, den_cols = [], []
        num_x = jnp.zeros((chunk, width), F32)
        for h in range(B_HEADS):
            lo = h * HEAD_DIM
            row = g_t[h:h + 1, :] - bc_t[B_HEADS + h:B_HEADS + h + 1, :]
            log_d = jnp.where(causal, bc_x[:, lo:lo + 1] + row, NEG_INF)
            m_loc = jnp.max(log_d, axis=-1, keepdims=True)
            s = lax.dot_general(jnp.where(head == h, qb, jnp.zeros_like(qb)), kb, (((1,), (1,)), ((), ())),
                                preferred_element_type=F32)
            sqk = s * jnp.exp(log_d - m_loc)
            num_x = jnp.where(head == h, jnp.dot(sqk.astype(BF16), vb, preferred_element_type=F32), num_x)
            m_cols.append(m_loc)
            den_cols.append(jnp.sum(sqk, axis=-1, keepdims=True))

        g_x = bc_x[chunk - 1:chunk, :]
        a_x = g_x - bc_x + i_all[r0:r0 + chunk]
        a_max = jnp.max(a_x, axis=0, keepdims=True)
        kw = k * jnp.exp(a_x - a_max)
        upd = lax.dot_general(kw.astype(BF16), vb, (((0,), (0,)), ((), ())), preferred_element_type=F32)
        local.append((q, qb, bc_x, _expand_heads(m_cols, head), num_x, _expand_heads(den_cols, head), g_x, a_max,
                      jnp.where(blockdiag, upd, 0.0), jnp.sum(kw, axis=0, keepdims=True)))

    for c, (q, qb, bc_x, m_loc_x, num_x, den_x, g_x, a_max, upd, k_sum) in enumerate(local):
        r0 = c * chunk
        c_prev, n_prev, m_prev = c_st[...], n_st[...], m_st[...]
        m_inter = bc_x + m_prev
        m_out_x = jnp.maximum(m_inter, m_loc_x)
        intra_w = jnp.exp(m_loc_x - m_out_x)
        inter_w = jnp.exp(m_inter - m_out_x)
        q_c = jnp.dot(qb, c_prev.astype(BF16), preferred_element_type=F32)
        q_n = _sum_select(q * n_prev, grp, parts=2)
        num = intra_w * num_x + inter_w * q_c
        den = intra_w * den_x + inter_w * q_n
        o_ref[r0:r0 + chunk, :] = (num / jnp.maximum(jnp.abs(den), jnp.exp(-m_out_x))).astype(BF16)

        m_new = jnp.maximum(g_x + m_prev, a_max)
        decay = jnp.exp(g_x + m_prev - m_new)
        fresh = jnp.exp(a_max - m_new)
        c_st[...] = decay * c_prev + fresh * upd
        n_st[...] = decay * n_prev + fresh * k_sum
        m_st[...] = m_new


def _mlstm(rest, gates, conv_w, tri_l, tri_u, grp, spread, bsz, t):
    n = rest.shape[0]
    tc, nt = TC_MLSTM, t // TC_MLSTM
    hr = BF16_ROWS
    const = lambda b, j: (0, 0)
    return pl.pallas_call(
        functools.partial(_mlstm_kernel, tc=tc, chunk=L_MLSTM),
        grid=(bsz, nt),
        in_specs=[
            pl.BlockSpec((tc, 2 * B_DIM), lambda b, j: (b * nt + j, 0)),
            pl.BlockSpec((hr, 2 * B_DIM), lambda b, j: (jnp.maximum((b * nt + j) * (tc // hr) - 1, 0), 0)),
            pl.BlockSpec((tc, B_DIM), lambda b, j: (b * nt + j, 2)),
            pl.BlockSpec((tc, LANE), lambda b, j: (b * nt + j, 0)),
            pl.BlockSpec((MLSTM_CONV, 2 * B_DIM), const),
            pl.BlockSpec((L_MLSTM, L_MLSTM), const),
            pl.BlockSpec((L_MLSTM, L_MLSTM), const),
            pl.BlockSpec((B_DIM, B_DIM), const),
            pl.BlockSpec((LANE, 2 * B_DIM), const),
        ],
        out_specs=pl.BlockSpec((tc, B_DIM), lambda b, j: (b * nt + j, 0)),
        out_shape=jax.ShapeDtypeStruct((n, B_DIM), BF16),
        scratch_shapes=[
            pltpu.VMEM((tc + hr, 2 * B_DIM), F32),
            pltpu.VMEM((B_DIM, B_DIM), F32),
            pltpu.VMEM((1, B_DIM), F32),
            pltpu.VMEM((1, B_DIM), F32),
        ],
        compiler_params=pltpu.CompilerParams(dimension_semantics=("arbitrary", "arbitrary"),
                                             vmem_limit_bytes=VMEM_LIMIT),
        name="mlstm",
    )(rest, rest, rest, gates, conv_w, tri_l, tri_u, grp, spread)


def _hgrn_kernel(q_ref, f_ref, i_ref, la_ref, l1m_ref, tri_ref, tot_ref, o_ref, st_ref, *, tc, chunk):
    nb = q_ref.shape[0]
    width = C_DIM

    @pl.when(pl.program_id(0) == 0)
    def _():
        st_ref[...] = jnp.zeros_like(st_ref)

    head = _head_of_lane((tc, width))
    head_sq = _head_of_lane((width, width))
    blockdiag = head_sq == lax.broadcasted_iota(jnp.int32, (width, width), 0) // HEAD_DIM
    rr = lax.broadcasted_iota(jnp.int32, (tc, tc), 0)
    cc = lax.broadcasted_iota(jnp.int32, (tc, tc), 1)
    keep = (cc <= rr) & (cc // chunk == rr // chunk)

    seqs = []
    for s in range(nb):
        xq = q_ref[s].astype(F32)
        q = xq * _sigmoid(xq)
        xf = f_ref[s].astype(F32)
        ls = _log_sigmoid(xf)
        a = la_ref[...]
        b = l1m_ref[...] + ls
        logf = jnp.maximum(a, b) + jnp.log(1.0 + jnp.exp(-jnp.abs(a - b)))
        kk = jnp.exp(l1m_ref[...] + ls - xf)
        vb = i_ref[s]

        bc = _select_sum(tri_ref[...], logf)
        blast = _select_sum(tot_ref[...], logf)
        qt = (q * jnp.exp(bc)).astype(BF16)
        kt = (kk * jnp.exp(-bc)).astype(BF16)
        kd = (kk * jnp.exp(blast - bc)).astype(BF16)

        intra = jnp.zeros((tc, width), F32)
        for h in range(C_HEADS):
            att = lax.dot_general(jnp.where(head == h, qt, jnp.zeros_like(qt)), kt, (((1,), (1,)), ((), ())),
                                  preferred_element_type=F32)
            att = jnp.where(keep, att, 0.0).astype(BF16)
            intra = jnp.where(head == h, jnp.dot(att, vb, preferred_element_type=F32), intra)
        seqs.append((qt, kd, vb, jnp.exp(blast), intra))

    for c in range(tc // chunk):
        r0 = c * chunk
        for s, (qt, kd, vb, chunk_decay, intra) in enumerate(seqs):
            s_t = st_ref[s]
            inter = lax.dot_general(qt[r0:r0 + chunk], s_t.astype(BF16), (((1,), (1,)), ((), ())),
                                    preferred_element_type=F32)
            o_ref[s, r0:r0 + chunk, :] = (intra[r0:r0 + chunk] + inter).astype(BF16)
            upd = lax.dot_general(vb[r0:r0 + chunk], kd[r0:r0 + chunk], (((0,), (0,)), ((), ())),
                                  preferred_element_type=F32)
            st_ref[s] = s_t * chunk_decay[r0:r0 + 1, :] + jnp.where(blockdiag, upd, 0.0)


def _hgrn(rest, log_lb, log_1m_lb, tri_bd, tot_bd, bsz, t):
    tc, nt = TC_HGRN, t // TC_HGRN
    rest3 = rest.reshape(bsz, t, REST_W)
    blk = lambda col: pl.BlockSpec((bsz, tc, C_DIM), lambda j: (0, j, col))
    const = lambda j: (0, 0)
    out = pl.pallas_call(
        functools.partial(_hgrn_kernel, tc=tc, chunk=L_HGRN),
        grid=(nt,),
        in_specs=[
            blk(4), blk(5), blk(6),
            pl.BlockSpec((1, C_DIM), const),
            pl.BlockSpec((1, C_DIM), const),
            pl.BlockSpec((tc, tc), const),
            pl.BlockSpec((tc, tc), const),
        ],
        out_specs=pl.BlockSpec((bsz, tc, C_DIM), lambda j: (0, j, 0)),
        out_shape=jax.ShapeDtypeStruct((bsz, t, C_DIM), BF16),
        scratch_shapes=[pltpu.VMEM((bsz, C_DIM, C_DIM), F32)],
        compiler_params=pltpu.CompilerParams(dimension_semantics=("arbitrary",), vmem_limit_bytes=VMEM_LIMIT),
        name="hgrn2",
    )(rest3, rest3, rest3, log_lb, log_1m_lb, tri_bd, tot_bd)
    return out.reshape(bsz * t, C_DIM)


def _outproj_kernel(x_ref, oa_ref, ob_ref, oc_ref, bo_ref, cg_ref, g_ref, grp_ref, w_ref, o_ref):
    def normed(o, lo, hi):
        return o * lax.rsqrt(_group_mean_sq(o, grp_ref) + EPS) * g_ref[:, lo:hi]

    ya = normed(oa_ref[...].astype(F32), 0, A_DIM)
    yb = _sigmoid(bo_ref[...].astype(F32)) * normed(ob_ref[...].astype(F32), A_DIM, A_DIM + B_DIM)
    cg = cg_ref[...].astype(F32)
    yc = cg * _sigmoid(cg) * normed(oc_ref[...].astype(F32), A_DIM + B_DIM, D_MIX)
    acc = x_ref[...] + jnp.dot(ya.astype(BF16), w_ref[0:A_DIM, :], preferred_element_type=F32)
    acc = acc + jnp.dot(yb.astype(BF16), w_ref[A_DIM:A_DIM + B_DIM, :], preferred_element_type=F32)
    acc = acc + jnp.dot(yc.astype(BF16), w_ref[A_DIM + B_DIM:D_MIX, :], preferred_element_type=F32)
    o_ref[...] = acc


def _outproj(x2, oa, ob, oc, rest, out_g, grp, w_out):
    n = x2.shape[0]
    tm = TM_OUT
    const = lambda i: (0, 0)
    return pl.pallas_call(
        _outproj_kernel,
        grid=(n // tm,),
        in_specs=[
            pl.BlockSpec((tm, D_MODEL), lambda i: (i, 0)),
            pl.BlockSpec((tm, A_DIM), lambda i: (i, 0)),
            pl.BlockSpec((tm, B_DIM), lambda i: (i, 0)),
            pl.BlockSpec((tm, C_DIM), lambda i: (i, 0)),
            pl.BlockSpec((tm, B_DIM), lambda i: (i, 3)),
            pl.BlockSpec((tm, C_DIM), lambda i: (i, 7)),
            pl.BlockSpec((1, D_MIX), const),
            pl.BlockSpec((A_DIM, A_DIM), const),
            pl.BlockSpec((D_MIX, D_MODEL), const),
        ],
        out_specs=pl.BlockSpec((tm, D_MODEL), lambda i: (i, 0)),
        out_shape=jax.ShapeDtypeStruct((n, D_MODEL), F32),
        compiler_params=pltpu.CompilerParams(dimension_semantics=("arbitrary",), vmem_limit_bytes=VMEM_LIMIT),
        name="outproj",
    )(x2, oa, ob, oc, rest, rest, out_g, grp, w_out)


def _ffn_kernel(x_ref, halo_ref, g_ref, wup_ref, cw_ref, cb_ref, wdn_ref, o_ref, h_ref, ug_ref, uu_ref, act_ref,
                *, tm, tiles_per_seq):
    i = pl.program_id(0)
    halo_rows = BF16_ROWS

    def rms(x):
        ms = jnp.mean(x * x, axis=-1, keepdims=True)
        return x * lax.rsqrt(ms + EPS) * g_ref[...]

    x = x_ref[...]
    first = (i % tiles_per_seq) == 0
    h_ref[0:halo_rows, :] = jnp.where(first, 0.0, rms(halo_ref[...])).astype(BF16)
    h_ref[halo_rows:, :] = rms(x).astype(BF16)
    h = h_ref[...]

    def conv(u_ref, slot, lo):
        y = cb_ref[:, lo:lo + FF_CHUNK] + cw_ref[FFN_CONV - 1:FFN_CONV, lo:lo + FF_CHUNK] * u_ref[slot, halo_rows:, :]
        for d in range(1, FFN_CONV):
            y = y + cw_ref[FFN_CONV - 1 - d:FFN_CONV - d, lo:lo + FF_CHUNK] * u_ref[slot, pl.ds(halo_rows - d, tm), :]
        return y

    def up_proj(c):
        lo = c * FF_CHUNK
        ug_ref[c % 2] = jnp.dot(h, wup_ref[:, lo:lo + FF_CHUNK], preferred_element_type=F32)
        uu_ref[c % 2] = jnp.dot(h, wup_ref[:, D_FF + lo:D_FF + lo + FF_CHUNK], preferred_element_type=F32)

    n_chunks = D_FF // FF_CHUNK
    up_proj(0)
    for c in range(n_chunks):
        if c + 1 < n_chunks:
            up_proj(c + 1)
        lo = c * FF_CHUNK
        gate = conv(ug_ref, c % 2, lo)
        up = conv(uu_ref, c % 2, D_FF + lo)
        act_ref[:, lo:lo + FF_CHUNK] = (gate * _sigmoid(gate) * up).astype(BF16)
    o_ref[...] = x + jnp.dot(act_ref[...], wdn_ref[...], preferred_element_type=F32)


def _ffn(x2, g, w_up, conv_w, conv_b, w_down, t):
    n = x2.shape[0]
    tm = TM_PROJ
    halo = BF16_ROWS
    const = lambda i: (0, 0)
    resident = pl.Buffered(1)
    return pl.pallas_call(
        functools.partial(_ffn_kernel, tm=tm, tiles_per_seq=t // tm),
        grid=(n // tm,),
        in_specs=[
            pl.BlockSpec((tm, D_MODEL), lambda i: (i, 0)),
            pl.BlockSpec((halo, D_MODEL), lambda i: (jnp.maximum(i * (tm // halo) - 1, 0), 0)),
            pl.BlockSpec((1, D_MODEL), const),
            pl.BlockSpec((D_MODEL, 2 * D_FF), const, pipeline_mode=resident),
            pl.BlockSpec((FFN_CONV, 2 * D_FF), const),
            pl.BlockSpec((1, 2 * D_FF), const),
            pl.BlockSpec((D_FF, D_MODEL), const, pipeline_mode=resident),
        ],
        out_specs=pl.BlockSpec((tm, D_MODEL), lambda i: (i, 0)),
        out_shape=jax.ShapeDtypeStruct((n, D_MODEL), F32),
        scratch_shapes=[
            pltpu.VMEM((tm + halo, D_MODEL), BF16),
            pltpu.VMEM((2, tm + halo, FF_CHUNK), F32),
            pltpu.VMEM((2, tm + halo, FF_CHUNK), F32),
            pltpu.VMEM((tm, D_FF), BF16),
        ],
        compiler_params=pltpu.CompilerParams(dimension_semantics=("arbitrary",), vmem_limit_bytes=VMEM_LIMIT),
        name="ffn",
    )(x2, x2, g, w_up, conv_w, conv_b, w_down)


def _block_diag_ones(n, blk, dtype):
    r = np.arange(n) // blk
    return jnp.asarray(r[:, None] == r[None, :], dtype)


def kernel(x, lb_logits, norm_mix_g, w_in, b_in, a_q_g, a_k_g, b_conv_w, out_g, w_out, norm_ffn_g, w_up,
           ffn_conv_w, ffn_conv_b, w_down):
    bsz, t, d = x.shape
    depth = w_in.shape[0]
    assert d == D_MODEL and t % TQ_FOX == 0 and t % TC_MLSTM == 0 and t % TC_HGRN == 0
    n = bsz * t
    x2 = x.reshape(n, d).astype(F32)

    p = jax.nn.softmax(lb_logits.astype(F32), axis=0)
    lb_all = jnp.maximum(jnp.cumsum(p, axis=0) - p[0], 0.0)
    log_lb = jnp.log(lb_all)
    log_1m_lb = jnp.log1p(-lb_all)

    grp_a = _block_diag_ones(A_DIM, HEAD_DIM, BF16)
    grp_b = _block_diag_ones(B_DIM, HEAD_DIM, BF16)
    ar = np.arange(L_MLSTM)
    tri_u = jnp.asarray(ar[:, None] <= ar[None, :], BF16)
    tri_l = jnp.asarray(ar[:, None] >= ar[None, :], BF16)
    gate_row = np.arange(LANE)[:, None] - A_HEADS
    spread = jnp.asarray(gate_row == np.arange(2 * B_DIM)[None, :] // HEAD_DIM, BF16)
    ap = np.arange(TM_PROJ)
    tri_proj = jnp.asarray(ap[:, None] >= ap[None, :], BF16)
    ah = np.arange(TC_HGRN)
    same = (ah[:, None] // L_HGRN) == (ah[None, :] // L_HGRN)
    tri_bd = jnp.asarray(same & (ah[:, None] >= ah[None, :]), BF16)
    tot_bd = jnp.asarray(same, BF16)

    w_in_r = _reorder_in_cols(w_in).astype(BF16)
    for l in range(depth):
        w_r = w_in_r[l]
        b_r = _reorder_in_cols(b_in[l].astype(F32))[None, :]
        gq = (jnp.tile(a_q_g[l].astype(F32), A_HEADS) * (HEAD_DIM ** -0.5 * LOG2E))[None, :]
        gk = jnp.tile(a_k_g[l].astype(F32), A_HEADS)[None, :]
        bound = HEAD_DIM * jnp.max(jnp.abs(gq)) * jnp.max(jnp.abs(gk))
        qa, ka, va, rest, gates, c_edges = _inproj(x2, norm_mix_g[l][None, :].astype(F32), w_r, b_r, gq, gk, grp_a, tri_proj,
                                          jnp.full((1, LANE), bound, F32), t)
        oa = lax.cond(bound < FOX_BOUND_MAX, functools.partial(_fox, bsz=bsz, t=t),
                      functools.partial(_fox_safe, bsz=bsz, t=t), qa, ka, va, c_edges)
        ob = _mlstm(rest, gates, b_conv_w[l].astype(F32), tri_l, tri_u, grp_b, spread, bsz, t)
        oc = _hgrn(rest, log_lb[l][None, :], log_1m_lb[l][None, :], tri_bd, tot_bd, bsz, t)
        x2 = _outproj(x2, oa, ob, oc, rest, out_g[l][None, :].astype(F32), grp_a, w_out[l].astype(BF16))
        x2 = _ffn(x2, norm_ffn_g[l][None, :].astype(F32), w_up[l].astype(BF16), ffn_conv_w[l].astype(F32),
                  ffn_conv_b[l][None, :].astype(F32), w_down[l].astype(BF16), t)
    return x2.reshape(bsz, t, d).astype(x.dtype)
```

```python
import functools

import numpy as np
import jax
import jax.numpy as jnp
from jax import lax
from jax.experimental import pallas as pl
from jax.experimental.pallas import tpu as pltpu

F32 = jnp.float32
BF16 = jnp.bfloat16
NEG_INF = float("-inf")
LOG2E = 1.4426950408889634

D_MODEL = 1024
HEAD_DIM = 64
A_HEADS, B_HEADS, C_HEADS = 8, 4, 4
A_DIM, B_DIM, C_DIM = A_HEADS * HEAD_DIM, B_HEADS * HEAD_DIM, C_HEADS * HEAD_DIM
D_MIX = A_DIM + B_DIM + C_DIM
D_FF = 2816
MLSTM_CONV = 4
FFN_CONV = 3
EPS = 1e-6

LANE = 128
BF16_ROWS = 16
VMEM_LIMIT = 56 * 1024 * 1024

QKV_W = 3 * A_DIM
REST_W = 8 * B_DIM
IN_W = QKV_W + REST_W + LANE

TM_PROJ = 512
TM_OUT = 1024
TQ_FOX = 1024
FOX_DIAG_BANDS = 2
FOX_BOUND_MAX = 40.0
FOX_SKIP = 160.0
TC_MLSTM = 512
L_MLSTM = 128
TC_HGRN = 256
L_HGRN = 16
HGRN_SAFE = 60.0
FF_CHUNK = 256


def _reorder_in_cols(w):
    a_f = 3 * A_DIM
    b_qk = a_f + A_HEADS
    b_v = b_qk + 2 * B_DIM
    b_i = b_v + B_DIM
    b_f = b_i + B_HEADS
    b_o = b_f + B_HEADS
    c_q = b_o + B_DIM
    end = c_q + 4 * C_DIM
    sl = lambda lo, hi: lax.slice_in_dim(w, lo, hi, axis=-1)
    n_gate = A_HEADS + 2 * B_HEADS
    pad = jnp.zeros(w.shape[:-1] + (LANE - n_gate,), w.dtype)
    out = jnp.concatenate([sl(0, a_f), sl(b_qk, b_i), sl(b_o, end), sl(a_f, b_qk), sl(b_i, b_o), pad], axis=-1)
    assert out.shape[-1] == IN_W
    return out


def _log_sigmoid(x):
    return jnp.minimum(x, 0.0) - jnp.log(1.0 + jnp.exp(-jnp.abs(x)))


def _sigmoid(x):
    return 1.0 / (1.0 + jnp.exp(-x))


def _head_of_lane(shape):
    return lax.broadcasted_iota(jnp.int32, shape, len(shape) - 1) // HEAD_DIM


def _expand_heads(cols, head):
    out = cols[-1]
    for h in range(len(cols) - 2, -1, -1):
        out = jnp.where(head == h, cols[h], out)
    return out


def _group_mean_sq(z, grp_ref):
    w = z.shape[-1]
    zz = (z * z).astype(BF16)
    return jnp.dot(zz, grp_ref[0:w, 0:w], preferred_element_type=F32) * (1.0 / HEAD_DIM)


def _split3(x):
    hi = x.astype(BF16).astype(F32)
    mid = (x - hi).astype(BF16).astype(F32)
    lo = (x - hi - mid).astype(BF16).astype(F32)
    return hi, mid, lo


def _select_sum(sel, x, parts=3):
    out = None
    for term in _split3(x)[:parts]:
        d = jnp.dot(sel, term.astype(BF16), preferred_element_type=F32)
        out = d if out is None else out + d
    return out


def _sum_select(x, sel, parts=3):
    out = None
    for term in _split3(x)[:parts]:
        d = jnp.dot(term.astype(BF16), sel, preferred_element_type=F32)
        out = d if out is None else out + d
    return out


def _inproj_kernel(x_ref, g_ref, w_ref, b_ref, gq_ref, gk_ref, grp_ref, tri_ref, bound_ref,
                   qa_ref, ka_ref, va_ref, rest_ref, gates_ref, edges_ref, carry_ref, *, tiles_per_seq):
    i = pl.program_id(0)
    tm = x_ref.shape[0]

    @pl.when(i % tiles_per_seq == 0)
    def _():
        carry_ref[...] = jnp.zeros_like(carry_ref)

    x = x_ref[...]
    ms = jnp.mean(x * x, axis=-1, keepdims=True)
    h = (x * lax.rsqrt(ms + EPS) * g_ref[...]).astype(BF16)

    def proj(lo, hi):
        return jnp.dot(h, w_ref[:, lo:hi], preferred_element_type=F32) + b_ref[:, lo:hi]

    gates = proj(QKV_W + REST_W, IN_W)
    gates_ref[...] = gates
    q = proj(0, A_DIM)
    k = proj(A_DIM, 2 * A_DIM)
    v = proj(2 * A_DIM, QKV_W)

    cs = _select_sum(tri_ref[...], _log_sigmoid(gates))
    c2 = (cs + carry_ref[...]) * LOG2E
    carry_ref[...] = carry_ref[...] + cs[tm - 1:tm, :]
    edge_row = lax.broadcasted_iota(jnp.int32, (8, LANE), 0)
    edges_ref[0] = jnp.where(edge_row == 0, c2[0:1, :], jnp.where(edge_row == 1, c2[tm - 1:tm, :], 0.0))

    q = q * lax.rsqrt(_group_mean_sq(q, grp_ref) + EPS) * gq_ref[...]
    k = k * lax.rsqrt(_group_mean_sq(k, grp_ref) + EPS) * gk_ref[...]
    for c in range(REST_W // A_DIM):
        rest_ref[:, c * A_DIM:(c + 1) * A_DIM] = proj(QKV_W + c * A_DIM, QKV_W + (c + 1) * A_DIM).astype(BF16)

    lane = lax.broadcasted_iota(jnp.int32, (tm, LANE), 1)
    lo_slot = (lane >= HEAD_DIM) & (lane < HEAD_DIM + 3)
    hi_slot = (lane >= HEAD_DIM + 3) & (lane < HEAD_DIM + 6)
    last = lane == HEAD_DIM + 6
    base_q = jnp.where(lo_slot, -1.0, jnp.where(last, 1.0, 0.0))
    base_k = jnp.where(hi_slot, 1.0, jnp.where(last, -bound_ref[...], 0.0))
    ones_v = jnp.where(lane == HEAD_DIM, 1.0, 0.0)
    data = lane < HEAD_DIM
    term = lane % 3
    assert HEAD_DIM % 3 == 1
    for hd in range(A_HEADS):
        pair = slice((hd // 2) * LANE, (hd // 2 + 1) * LANE)
        out = slice(hd * LANE, (hd + 1) * LANE)
        place = (lambda z: z) if hd % 2 == 0 else (lambda z: pltpu.roll(z, HEAD_DIM, 1))
        c_hi, c_mid, c_lo = _split3(jnp.broadcast_to(c2[:, hd:hd + 1], (tm, LANE)))
        terms = jnp.where(term == 1, c_hi, jnp.where(term == 2, c_mid, c_lo))
        aug_q = jnp.where(hi_slot, terms, base_q)
        aug_k = jnp.where(lo_slot, terms, base_k)
        qa_ref[:, out] = jnp.where(data, place(q[:, pair]), aug_q).astype(BF16)
        ka_ref[:, out] = jnp.where(data, place(k[:, pair]), aug_k).astype(BF16)
        va_ref[:, out] = jnp.where(data, place(v[:, pair]), ones_v).astype(BF16)


def _inproj(x2, g, w, b, gq, gk, grp, tri, bound, t):
    n = x2.shape[0]
    tm = TM_PROJ
    aug_w = A_HEADS * LANE
    const = lambda i: (0, 0)
    row = lambda width: pl.BlockSpec((tm, width), lambda i: (i, 0))
    return pl.pallas_call(
        functools.partial(_inproj_kernel, tiles_per_seq=t // tm),
        grid=(n // tm,),
        in_specs=[
            row(D_MODEL),
            pl.BlockSpec((1, D_MODEL), const),
            pl.BlockSpec((D_MODEL, IN_W), const),
            pl.BlockSpec((1, IN_W), const),
            pl.BlockSpec((1, A_DIM), const),
            pl.BlockSpec((1, A_DIM), const),
            pl.BlockSpec((A_DIM, A_DIM), const),
            pl.BlockSpec((tm, tm), const),
            pl.BlockSpec((1, LANE), const),
        ],
        out_specs=[row(aug_w), row(aug_w), row(aug_w), row(REST_W), row(LANE),
                   pl.BlockSpec((1, 8, LANE), lambda i: (i, 0, 0))],
        out_shape=[
            jax.ShapeDtypeStruct((n, aug_w), BF16),
            jax.ShapeDtypeStruct((n, aug_w), BF16),
            jax.ShapeDtypeStruct((n, aug_w), BF16),
            jax.ShapeDtypeStruct((n, REST_W), BF16),
            jax.ShapeDtypeStruct((n, LANE), F32),
            jax.ShapeDtypeStruct((n // tm, 8, LANE), F32),
        ],
        scratch_shapes=[pltpu.VMEM((1, LANE), F32)],
        compiler_params=pltpu.CompilerParams(dimension_semantics=("arbitrary",), vmem_limit_bytes=VMEM_LIMIT),
        name="inproj",
    )(x2, g, w, b, gq, gk, grp, tri, bound)


def _fox_kernel(first_ref, q_ref, k_ref, v_ref, o_ref, *, tq):
    i = pl.program_id(2)
    step = (pl.program_id(0) * pl.num_programs(1) + pl.program_id(1)) * pl.num_programs(2) + i
    nt = (((1,), (1,)), ((), ()))
    half = tq // 2

    def attend(q, keys, hh, visible=None):
        cols = slice(hh * LANE, (hh + 1) * LANE)
        s = lax.dot_general(q, k_ref[keys, cols], nt, preferred_element_type=F32)
        if visible is not None:
            s = jnp.where(visible, s, NEG_INF)
        return jnp.dot(jnp.exp2(s).astype(BF16), v_ref[keys, cols], preferred_element_type=F32)

    def full_step(j, accs):
        keys = pl.ds(pl.multiple_of(j * tq, tq), tq)
        return tuple(accs[hh] + attend(q_ref[:, hh * LANE:(hh + 1) * LANE], keys, hh) for hh in range(2))

    dead_halves = first_ref[step]

    def half_step():
        keys = pl.ds(pl.multiple_of(dead_halves * half, half), half)
        return tuple(attend(q_ref[:, hh * LANE:(hh + 1) * LANE], keys, hh) for hh in range(2))

    zero = jnp.zeros((tq, LANE), F32)
    accs = lax.cond(dead_halves % 2 == 1, half_step, lambda: (zero, zero))
    accs = lax.fori_loop((dead_halves + 1) // 2, i, full_step, accs)

    off = pl.multiple_of(i * tq, tq)
    band = tq // FOX_DIAG_BANDS
    lane = lax.broadcasted_iota(jnp.int32, (band, LANE), 1)
    for r in range(FOX_DIAG_BANDS):
        rows = slice(r * band, (r + 1) * band)
        n_keys = (r + 1) * band
        visible = (lax.broadcasted_iota(jnp.int32, (band, n_keys), 1)
                   <= lax.broadcasted_iota(jnp.int32, (band, n_keys), 0) + r * band)
        outs = []
        for hh in range(2):
            cols = slice(hh * LANE, (hh + 1) * LANE)
            acc = accs[hh][rows] + attend(q_ref[rows, cols], pl.ds(off, n_keys), hh, visible)
            outs.append(acc / acc[:, HEAD_DIM:HEAD_DIM + 1])
        o_ref[rows, :] = jnp.where(lane < HEAD_DIM, outs[0], pltpu.roll(outs[1], HEAD_DIM, 1)).astype(BF16)


def _fox(qa, ka, va, c_edges, bsz, t):
    n = qa.shape[0]
    tq = TQ_FOX
    nq = t // tq
    pairs = A_HEADS // 2

    assert tq == 2 * TM_PROJ
    edges = c_edges.reshape(bsz, 2 * nq, 8, LANE)
    c_first = edges[:, 0::2, 0, :A_HEADS]
    c_last = edges[:, :, 1, :A_HEADS]
    gap = c_first[:, :, None, :] - c_last[:, None, :, :]
    dead = jnp.all((gap < -FOX_SKIP).reshape(bsz, nq, 2 * nq, pairs, 2), axis=-1)
    dead = dead & (2 * jnp.arange(nq)[None, :, None, None] > jnp.arange(2 * nq)[None, None, :, None])
    first = jnp.sum(dead, axis=2).astype(jnp.int32).transpose(0, 2, 1).reshape(-1)

    return pl.pallas_call(
        functools.partial(_fox_kernel, tq=tq),
        grid_spec=pltpu.PrefetchScalarGridSpec(
            num_scalar_prefetch=1,
            grid=(bsz, pairs, nq),
            in_specs=[
                pl.BlockSpec((tq, 2 * LANE), lambda b, p, i, first: (b * nq + i, p)),
                pl.BlockSpec((t, 2 * LANE), lambda b, p, i, first: (b, p)),
                pl.BlockSpec((t, 2 * LANE), lambda b, p, i, first: (b, p)),
            ],
            out_specs=pl.BlockSpec((tq, LANE), lambda b, p, i, first: (b * nq + i, p)),
        ),
        out_shape=jax.ShapeDtypeStruct((n, A_DIM), BF16),
        compiler_params=pltpu.CompilerParams(
            dimension_semantics=("arbitrary", "arbitrary", "arbitrary"), vmem_limit_bytes=VMEM_LIMIT),
        name="fox",
    )(first, qa, ka, va)


def _fox_safe_kernel(q_ref, k_ref, v_ref, c_ref, o_ref, *, tq):
    i = pl.program_id(2)
    q = q_ref[...]
    lane = lax.broadcasted_iota(jnp.int32, (tq, LANE), 1)
    zero = jnp.zeros_like(q)
    qh = (jnp.where(lane < HEAD_DIM, q, zero), jnp.where(lane >= HEAD_DIM, q, zero))
    tri = lax.broadcasted_iota(jnp.int32, (tq, tq), 1) <= lax.broadcasted_iota(jnp.int32, (tq, tq), 0)

    def step(j, carry, masked):
        off = pl.multiple_of(j * tq, tq)
        kt = k_ref[pl.ds(off, tq), :]
        vt = v_ref[pl.ds(off, tq), :]
        new = []
        for hh in range(2):
            m, l, acc = carry[hh]
            s = lax.dot_general(qh[hh], kt, (((1,), (1,)), ((), ())), preferred_element_type=F32)
            s = s - c_ref[0, hh:hh + 1, pl.ds(off, tq)]
            if masked:
                s = jnp.where(tri, s, NEG_INF)
            m_new = jnp.maximum(m, jnp.max(s, axis=-1, keepdims=True))
            alpha = jnp.exp2(m - m_new)
            p = jnp.exp2(s - m_new)
            l = alpha * l + jnp.sum(p, axis=-1, keepdims=True)
            acc = alpha * acc + jnp.dot(p.astype(BF16), vt, preferred_element_type=F32)
            new.append((m_new, l, acc))
        return tuple(new)

    init = tuple((jnp.full((tq, 1), NEG_INF, F32), jnp.zeros((tq, 1), F32), jnp.zeros((tq, LANE), F32))
                 for _ in range(2))
    carry = lax.fori_loop(0, i, lambda j, c: step(j, c, False), init)
    (_, l0, a0), (_, l1, a1) = step(i, carry, True)
    o_ref[...] = jnp.where(lane < HEAD_DIM, a0 / l0, a1 / l1).astype(BF16)


def _fox_safe(qa, ka, va, c_edges, bsz, t):
    del c_edges
    n = qa.shape[0]
    tq = TQ_FOX // 2
    nq = t // tq
    pairs = A_HEADS // 2
    heads = lambda z: z.reshape(n, A_HEADS, LANE)
    unpack = lambda z: heads(z)[:, :, :HEAD_DIM].reshape(n, A_DIM)
    c = jnp.sum(heads(ka)[:, :, HEAD_DIM:HEAD_DIM + 3].astype(F32), axis=-1)
    c_t = c.reshape(bsz, t, A_HEADS).transpose(0, 2, 1).reshape(bsz * pairs, 2, t)
    return pl.pallas_call(
        functools.partial(_fox_safe_kernel, tq=tq),
        grid=(bsz, pairs, nq),
        in_specs=[
            pl.BlockSpec((tq, LANE), lambda b, p, i: (b * nq + i, p)),
            pl.BlockSpec((t, LANE), lambda b, p, i: (b, p)),
            pl.BlockSpec((t, LANE), lambda b, p, i: (b, p)),
            pl.BlockSpec((1, 2, t), lambda b, p, i: (b * pairs + p, 0, 0)),
        ],
        out_specs=pl.BlockSpec((tq, LANE), lambda b, p, i: (b * nq + i, p)),
        out_shape=jax.ShapeDtypeStruct((n, A_DIM), BF16),
        compiler_params=pltpu.CompilerParams(
            dimension_semantics=("arbitrary", "arbitrary", "arbitrary"), vmem_limit_bytes=VMEM_LIMIT),
        name="fox_safe",
    )(unpack(qa), unpack(ka), unpack(va), c_t)


def _mlstm_kernel(qk_ref, halo_ref, v_ref, gates_ref, cw_ref, tri_ref, triu_ref, grp_ref, spread_ref, o_ref,
                  buf_ref, c_st, n_st, m_st, *, tc, chunk):
    j = pl.program_id(1)
    width = B_DIM

    @pl.when(j == 0)
    def _():
        c_st[...] = jnp.zeros_like(c_st)
        n_st[...] = jnp.zeros_like(n_st)
        m_st[...] = jnp.full(m_st.shape, NEG_INF, F32)

    hr = BF16_ROWS
    buf_ref[0:hr, :] = jnp.where(j == 0, 0.0, halo_ref[...].astype(F32))
    buf_ref[hr:hr + tc, :] = qk_ref[...].astype(F32)
    cw = cw_ref[...]
    y = cw[MLSTM_CONV - 1:MLSTM_CONV] * buf_ref[hr:hr + tc, :]
    for d in range(1, MLSTM_CONV):
        y = y + cw[MLSTM_CONV - 1 - d:MLSTM_CONV - d] * buf_ref[hr - d:hr - d + tc, :]
    y = y * _sigmoid(y)
    q_all = y[:, :width]
    k_all = y[:, width:] * (HEAD_DIM ** -0.5)

    head = _head_of_lane((chunk, width))
    head_sq = _head_of_lane((width, width))
    blockdiag = head_sq == lax.broadcasted_iota(jnp.int32, (width, width), 0) // HEAD_DIM
    causal = lax.broadcasted_iota(jnp.int32, (chunk, chunk), 1) <= lax.broadcasted_iota(jnp.int32, (chunk, chunk), 0)
    tri = tri_ref[...]
    grp = grp_ref[...]

    gx = _sum_select(gates_ref[...], spread_ref[...])
    i_all = gx[:, :width]
    f_all = gx[:, width:]

    local = []
    for c in range(tc // chunk):
        r0 = c * chunk
        q = q_all[r0:r0 + chunk]
        k = k_all[r0:r0 + chunk]
        vb = v_ref[r0:r0 + chunk, :]
        qb, kb = q.astype(BF16), k.astype(BF16)
        bc_x = _select_sum(tri, _log_sigmoid(f_all[r0:r0 + chunk]))
        g_t = gates_ref[r0:r0 + chunk, :].T[A_HEADS:A_HEADS + 2 * B_HEADS, :]
        bc_t = _sum_select(_log_sigmoid(g_t), triu_ref[...])

        m_cols, den_cols = [], []
        num_x = jnp.zeros((chunk, width), F32)
        for h in range(B_HEADS):
            lo = h * HEAD_DIM
            row = g_t[h:h + 1, :] - bc_t[B_HEADS + h:B_HEADS + h + 1, :]
            log_d = jnp.where(causal, bc_x[:, lo:lo + 1] + row, NEG_INF)
            m_loc = jnp.max(log_d, axis=-1, keepdims=True)
            s = lax.dot_general(jnp.where(head == h, qb, jnp.zeros_like(qb)), kb, (((1,), (1,)), ((), ())),
                                preferred_element_type=F32)
            sqk = s * jnp.exp(log_d - m_loc)
            num_x = jnp.where(head == h, jnp.dot(sqk.astype(BF16), vb, preferred_element_type=F32), num_x)
            m_cols.append(m_loc)
            den_cols.append(jnp.sum(sqk, axis=-1, keepdims=True))

        g_x = bc_x[chunk - 1:chunk, :]
        a_x = g_x - bc_x + i_all[r0:r0 + chunk]
        a_max = jnp.max(a_x, axis=0, keepdims=True)
        kw = k * jnp.exp(a_x - a_max)
        upd = lax.dot_general(kw.astype(BF16), vb, (((0,), (0,)), ((), ())), preferred_element_type=F32)
        local.append((q, qb, bc_x, _expand_heads(m_cols, head), num_x, _expand_heads(den_cols, head), g_x, a_max,
                      jnp.where(blockdiag, upd, 0.0), jnp.sum(kw, axis=0, keepdims=True)))

    for c, (q, qb, bc_x, m_loc_x, num_x, den_x, g_x, a_max, upd, k_sum) in enumerate(local):
        r0 = c * chunk
        c_prev, n_prev, m_prev = c_st[...], n_st[...], m_st[...]
        m_inter = bc_x + m_prev
        m_out_x = jnp.maximum(m_inter, m_loc_x)
        intra_w = jnp.exp(m_loc_x - m_out_x)
        inter_w = jnp.exp(m_inter - m_out_x)
        q_c = jnp.dot(qb, c_prev.astype(BF16), preferred_element_type=F32)
        q_n = _sum_select(q * n_prev, grp, parts=2)
        num = intra_w * num_x + inter_w * q_c
        den = intra_w * den_x + inter_w * q_n
        o_ref[r0:r0 + chunk, :] = (num / jnp.maximum(jnp.abs(den), jnp.exp(-m_out_x))).astype(BF16)

        m_new = jnp.maximum(g_x + m_prev, a_max)
        decay = jnp.exp(g_x + m_prev - m_new)
        fresh = jnp.exp(a_max - m_new)
        c_st[...] = decay * c_prev + fresh * upd
        n_st[...] = decay * n_prev + fresh * k_sum
        m_st[...] = m_new


def _mlstm(rest, gates, conv_w, tri_l, tri_u, grp, spread, bsz, t):
    n = rest.shape[0]
    tc, nt = TC_MLSTM, t // TC_MLSTM
    hr = BF16_ROWS
    const = lambda b, j: (0, 0)
    return pl.pallas_call(
        functools.partial(_mlstm_kernel, tc=tc, chunk=L_MLSTM),
        grid=(bsz, nt),
        in_specs=[
            pl.BlockSpec((tc, 2 * B_DIM), lambda b, j: (b * nt + j, 0)),
            pl.BlockSpec((hr, 2 * B_DIM), lambda b, j: (jnp.maximum((b * nt + j) * (tc // hr) - 1, 0), 0)),
            pl.BlockSpec((tc, B_DIM), lambda b, j: (b * nt + j, 2)),
            pl.BlockSpec((tc, LANE), lambda b, j: (b * nt + j, 0)),
            pl.BlockSpec((MLSTM_CONV, 2 * B_DIM), const),
            pl.BlockSpec((L_MLSTM, L_MLSTM), const),
            pl.BlockSpec((L_MLSTM, L_MLSTM), const),
            pl.BlockSpec((B_DIM, B_DIM), const),
            pl.BlockSpec((LANE, 2 * B_DIM), const),
        ],
        out_specs=pl.BlockSpec((tc, B_DIM), lambda b, j: (b * nt + j, 0)),
        out_shape=jax.ShapeDtypeStruct((n, B_DIM), BF16),
        scratch_shapes=[
            pltpu.VMEM((tc + hr, 2 * B_DIM), F32),
            pltpu.VMEM((B_DIM, B_DIM), F32),
            pltpu.VMEM((1, B_DIM), F32),
            pltpu.VMEM((1, B_DIM), F32),
        ],
        compiler_params=pltpu.CompilerParams(dimension_semantics=("arbitrary", "arbitrary"),
                                             vmem_limit_bytes=VMEM_LIMIT),
        name="mlstm",
    )(rest, rest, rest, gates, conv_w, tri_l, tri_u, grp, spread)


def _hgrn_kernel(q_ref, f_ref, i_ref, la_ref, l1m_ref, tri_ref, tot_ref, grp_ref, o_ref, low_ref, st_ref,
                 *, tc, chunk, direct):
    nb = q_ref.shape[0]
    width = C_DIM

    @pl.when(pl.program_id(0) == 0)
    def _():
        st_ref[...] = jnp.zeros_like(st_ref)

    head = _head_of_lane((tc, width))
    head_sq = _head_of_lane((width, width))
    blockdiag = head_sq == lax.broadcasted_iota(jnp.int32, (width, width), 0) // HEAD_DIM
    rr = lax.broadcasted_iota(jnp.int32, (tc, tc), 0)
    cc = lax.broadcasted_iota(jnp.int32, (tc, tc), 1)
    keep = (cc <= rr) & (cc // chunk == rr // chunk)

    seqs = []
    low = None
    for s in range(nb):
        xq = q_ref[s].astype(F32)
        q = xq * _sigmoid(xq)
        xf = f_ref[s].astype(F32)
        ls = _log_sigmoid(xf)
        a = la_ref[...]
        b = l1m_ref[...] + ls
        logf = jnp.maximum(a, b) + jnp.log(1.0 + jnp.exp(-jnp.abs(a - b)))
        kk = jnp.exp(l1m_ref[...] + ls - xf)
        vb = i_ref[s]

        bc = _select_sum(tri_ref[...], logf)
        blast = _select_sum(tot_ref[...], logf)
        qt = (q * jnp.exp(bc)).astype(BF16)
        kd = (kk * jnp.exp(blast - bc)).astype(BF16)

        def factored_intra(q=q, kk=kk, bc=bc, qt=qt, vb=vb):
            kt = (kk * jnp.exp(-bc)).astype(BF16)
            intra = jnp.zeros((tc, width), F32)
            for h in range(C_HEADS):
                att = lax.dot_general(jnp.where(head == h, qt, jnp.zeros_like(qt)), kt, (((1,), (1,)), ((), ())),
                                      preferred_element_type=F32)
                att = jnp.where(keep, att, 0.0).astype(BF16)
                intra = jnp.where(head == h, jnp.dot(att, vb, preferred_element_type=F32), intra)
            return intra

        def direct_intra(q=q, kk=kk, bc=bc, vb=vb):
            v = vb.astype(F32)
            pos = lax.broadcasted_iota(jnp.int32, (tc, width), 0) % chunk
            intra = jnp.zeros((tc, width), F32)
            for off in range(chunk):
                back = (lambda z: z) if off == 0 else (lambda z: pltpu.roll(z, off, 0))
                w = jnp.where(pos >= off, q * back(kk) * jnp.exp(jnp.minimum(bc - back(bc), 0.0)), 0.0)
                intra = intra + _sum_select(w, grp_ref[...]) * back(v)
            return intra

        intra = direct_intra() if direct else factored_intra()
        seqs.append((qt, kd, vb, jnp.exp(blast), intra))
        bc_min = jnp.min(bc, axis=(0, 1), keepdims=True)
        low = bc_min if low is None else jnp.minimum(low, bc_min)
    low_ref[0] = jnp.broadcast_to(low, low_ref.shape[1:])

    for c in range(tc // chunk):
        r0 = c * chunk
        for s, (qt, kd, vb, chunk_decay, intra) in enumerate(seqs):
            s_t = st_ref[s]
            inter = lax.dot_general(qt[r0:r0 + chunk], s_t.astype(BF16), (((1,), (1,)), ((), ())),
                                    preferred_element_type=F32)
            o_ref[s, r0:r0 + chunk, :] = (intra[r0:r0 + chunk] + inter).astype(BF16)
            upd = lax.dot_general(vb[r0:r0 + chunk], kd[r0:r0 + chunk], (((0,), (0,)), ((), ())),
                                  preferred_element_type=F32)
            st_ref[s] = s_t * chunk_decay[r0:r0 + 1, :] + jnp.where(blockdiag, upd, 0.0)


def _hgrn(rest, log_lb, log_1m_lb, tri_bd, tot_bd, grp, bsz, t):
    fast, low = _hgrn_call(rest, log_lb, log_1m_lb, tri_bd, tot_bd, grp, bsz, t, direct=False)
    redo = lambda: _hgrn_call(rest, log_lb, log_1m_lb, tri_bd, tot_bd, grp, bsz, t, direct=True)[0]
    return lax.cond(jnp.min(low) < -HGRN_SAFE, redo, lambda: fast)


def _hgrn_call(rest, log_lb, log_1m_lb, tri_bd, tot_bd, grp, bsz, t, direct):
    tc, nt = TC_HGRN, t // TC_HGRN
    rest3 = rest.reshape(bsz, t, REST_W)
    blk = lambda col: pl.BlockSpec((bsz, tc, C_DIM), lambda j: (0, j, col))
    const = lambda j: (0, 0)
    out, low = pl.pallas_call(
        functools.partial(_hgrn_kernel, tc=tc, chunk=L_HGRN, direct=direct),
        grid=(nt,),
        in_specs=[
            blk(4), blk(5), blk(6),
            pl.BlockSpec((1, C_DIM), const),
            pl.BlockSpec((1, C_DIM), const),
            pl.BlockSpec((tc, tc), const),
            pl.BlockSpec((tc, tc), const),
            pl.BlockSpec((C_DIM, C_DIM), const),
        ],
        out_specs=[pl.BlockSpec((bsz, tc, C_DIM), lambda j: (0, j, 0)),
                   pl.BlockSpec((1, 8, LANE), lambda j: (j, 0, 0))],
        out_shape=[jax.ShapeDtypeStruct((bsz, t, C_DIM), BF16), jax.ShapeDtypeStruct((nt, 8, LANE), F32)],
        scratch_shapes=[pltpu.VMEM((bsz, C_DIM, C_DIM), F32)],
        compiler_params=pltpu.CompilerParams(dimension_semantics=("arbitrary",), vmem_limit_bytes=VMEM_LIMIT),
        name="hgrn2_direct" if direct else "hgrn2",
    )(rest3, rest3, rest3, log_lb, log_1m_lb, tri_bd, tot_bd, grp)
    return out.reshape(bsz * t, C_DIM), low


def _outproj_kernel(x_ref, oa_ref, ob_ref, oc_ref, bo_ref, cg_ref, g_ref, grp_ref, w_ref, o_ref):
    def normed(o, lo, hi):
        return o * lax.rsqrt(_group_mean_sq(o, grp_ref) + EPS) * g_ref[:, lo:hi]

    ya = normed(oa_ref[...].astype(F32), 0, A_DIM)
    yb = _sigmoid(bo_ref[...].astype(F32)) * normed(ob_ref[...].astype(F32), A_DIM, A_DIM + B_DIM)
    cg = cg_ref[...].astype(F32)
    yc = cg * _sigmoid(cg) * normed(oc_ref[...].astype(F32), A_DIM + B_DIM, D_MIX)
    acc = x_ref[...] + jnp.dot(ya.astype(BF16), w_ref[0:A_DIM, :], preferred_element_type=F32)
    acc = acc + jnp.dot(yb.astype(BF16), w_ref[A_DIM:A_DIM + B_DIM, :], preferred_element_type=F32)
    acc = acc + jnp.dot(yc.astype(BF16), w_ref[A_DIM + B_DIM:D_MIX, :], preferred_element_type=F32)
    o_ref[...] = acc


def _outproj(x2, oa, ob, oc, rest, out_g, grp, w_out):
    n = x2.shape[0]
    tm = TM_OUT
    const = lambda i: (0, 0)
    return pl.pallas_call(
        _outproj_kernel,
        grid=(n // tm,),
        in_specs=[
            pl.BlockSpec((tm, D_MODEL), lambda i: (i, 0)),
            pl.BlockSpec((tm, A_DIM), lambda i: (i, 0)),
            pl.BlockSpec((tm, B_DIM), lambda i: (i, 0)),
            pl.BlockSpec((tm, C_DIM), lambda i: (i, 0)),
            pl.BlockSpec((tm, B_DIM), lambda i: (i, 3)),
            pl.BlockSpec((tm, C_DIM), lambda i: (i, 7)),
            pl.BlockSpec((1, D_MIX), const),
            pl.BlockSpec((A_DIM, A_DIM), const),
            pl.BlockSpec((D_MIX, D_MODEL), const),
        ],
        out_specs=pl.BlockSpec((tm, D_MODEL), lambda i: (i, 0)),
        out_shape=jax.ShapeDtypeStruct((n, D_MODEL), F32),
        compiler_params=pltpu.CompilerParams(dimension_semantics=("arbitrary",), vmem_limit_bytes=VMEM_LIMIT),
        name="outproj",
    )(x2, oa, ob, oc, rest, rest, out_g, grp, w_out)


def _ffn_kernel(x_ref, halo_ref, g_ref, wup_ref, cw_ref, cb_ref, wdn_ref, o_ref, h_ref, ug_ref, uu_ref, act_ref,
                *, tm, tiles_per_seq):
    i = pl.program_id(0)
    halo_rows = BF16_ROWS

    def rms(x):
        ms = jnp.mean(x * x, axis=-1, keepdims=True)
        return x * lax.rsqrt(ms + EPS) * g_ref[...]

    x = x_ref[...]
    first = (i % tiles_per_seq) == 0
    h_ref[0:halo_rows, :] = jnp.where(first, 0.0, rms(halo_ref[...])).astype(BF16)
    h_ref[halo_rows:, :] = rms(x).astype(BF16)
    h = h_ref[...]

    def conv(u_ref, slot, lo):
        y = cb_ref[:, lo:lo + FF_CHUNK] + cw_ref[FFN_CONV - 1:FFN_CONV, lo:lo + FF_CHUNK] * u_ref[slot, halo_rows:, :]
        for d in range(1, FFN_CONV):
            y = y + cw_ref[FFN_CONV - 1 - d:FFN_CONV - d, lo:lo + FF_CHUNK] * u_ref[slot, pl.ds(halo_rows - d, tm), :]
        return y

    def up_proj(c):
        lo = c * FF_CHUNK
        ug_ref[c % 2] = jnp.dot(h, wup_ref[:, lo:lo + FF_CHUNK], preferred_element_type=F32)
        uu_ref[c % 2] = jnp.dot(h, wup_ref[:, D_FF + lo:D_FF + lo + FF_CHUNK], preferred_element_type=F32)

    n_chunks = D_FF // FF_CHUNK
    up_proj(0)
    for c in range(n_chunks):
        if c + 1 < n_chunks:
            up_proj(c + 1)
        lo = c * FF_CHUNK
        gate = conv(ug_ref, c % 2, lo)
        up = conv(uu_ref, c % 2, D_FF + lo)
        act_ref[:, lo:lo + FF_CHUNK] = (gate * _sigmoid(gate) * up).astype(BF16)
    o_ref[...] = x + jnp.dot(act_ref[...], wdn_ref[...], preferred_element_type=F32)


def _ffn(x2, g, w_up, conv_w, conv_b, w_down, t):
    n = x2.shape[0]
    tm = TM_PROJ
    halo = BF16_ROWS
    const = lambda i: (0, 0)
    resident = pl.Buffered(1)
    return pl.pallas_call(
        functools.partial(_ffn_kernel, tm=tm, tiles_per_seq=t // tm),
        grid=(n // tm,),
        in_specs=[
            pl.BlockSpec((tm, D_MODEL), lambda i: (i, 0)),
            pl.BlockSpec((halo, D_MODEL), lambda i: (jnp.maximum(i * (tm // halo) - 1, 0), 0)),
            pl.BlockSpec((1, D_MODEL), const),
            pl.BlockSpec((D_MODEL, 2 * D_FF), const, pipeline_mode=resident),
            pl.BlockSpec((FFN_CONV, 2 * D_FF), const),
            pl.BlockSpec((1, 2 * D_FF), const),
            pl.BlockSpec((D_FF, D_MODEL), const, pipeline_mode=resident),
        ],
        out_specs=pl.BlockSpec((tm, D_MODEL), lambda i: (i, 0)),
        out_shape=jax.ShapeDtypeStruct((n, D_MODEL), F32),
        scratch_shapes=[
            pltpu.VMEM((tm + halo, D_MODEL), BF16),
            pltpu.VMEM((2, tm + halo, FF_CHUNK), F32),
            pltpu.VMEM((2, tm + halo, FF_CHUNK), F32),
            pltpu.VMEM((tm, D_FF), BF16),
        ],
        compiler_params=pltpu.CompilerParams(dimension_semantics=("arbitrary",), vmem_limit_bytes=VMEM_LIMIT),
        name="ffn",
    )(x2, x2, g, w_up, conv_w, conv_b, w_down)


def _block_diag_ones(n, blk, dtype):
    r = np.arange(n) // blk
    return jnp.asarray(r[:, None] == r[None, :], dtype)


def kernel(x, lb_logits, norm_mix_g, w_in, b_in, a_q_g, a_k_g, b_conv_w, out_g, w_out, norm_ffn_g, w_up,
           ffn_conv_w, ffn_conv_b, w_down):
    bsz, t, d = x.shape
    depth = w_in.shape[0]
    assert d == D_MODEL and t % TQ_FOX == 0 and t % TC_MLSTM == 0 and t % TC_HGRN == 0
    n = bsz * t
    x2 = x.reshape(n, d).astype(F32)

    p = jax.nn.softmax(lb_logits.astype(F32), axis=0)
    lb_all = jnp.maximum(jnp.cumsum(p, axis=0) - p[0], 0.0)
    log_lb = jnp.log(lb_all)
    log_1m_lb = jnp.log1p(-lb_all)

    grp_a = _block_diag_ones(A_DIM, HEAD_DIM, BF16)
    grp_b = _block_diag_ones(B_DIM, HEAD_DIM, BF16)
    ar = np.arange(L_MLSTM)
    tri_u = jnp.asarray(ar[:, None] <= ar[None, :], BF16)
    tri_l = jnp.asarray(ar[:, None] >= ar[None, :], BF16)
    gate_row = np.arange(LANE)[:, None] - A_HEADS
    spread = jnp.asarray(gate_row == np.arange(2 * B_DIM)[None, :] // HEAD_DIM, BF16)
    ap = np.arange(TM_PROJ)
    tri_proj = jnp.asarray(ap[:, None] >= ap[None, :], BF16)
    ah = np.arange(TC_HGRN)
    same = (ah[:, None] // L_HGRN) == (ah[None, :] // L_HGRN)
    tri_bd = jnp.asarray(same & (ah[:, None] >= ah[None, :]), BF16)
    tot_bd = jnp.asarray(same, BF16)

    w_in_r = _reorder_in_cols(w_in).astype(BF16)
    for l in range(depth):
        w_r = w_in_r[l]
        b_r = _reorder_in_cols(b_in[l].astype(F32))[None, :]
        gq = (jnp.tile(a_q_g[l].astype(F32), A_HEADS) * (HEAD_DIM ** -0.5 * LOG2E))[None, :]
        gk = jnp.tile(a_k_g[l].astype(F32), A_HEADS)[None, :]
        bound = HEAD_DIM * jnp.max(jnp.abs(gq)) * jnp.max(jnp.abs(gk))
        qa, ka, va, rest, gates, c_edges = _inproj(x2, norm_mix_g[l][None, :].astype(F32), w_r, b_r, gq, gk, grp_a, tri_proj,
                                          jnp.full((1, LANE), bound, F32), t)
        oa = lax.cond(bound < FOX_BOUND_MAX, functools.partial(_fox, bsz=bsz, t=t),
                      functools.partial(_fox_safe, bsz=bsz, t=t), qa, ka, va, c_edges)
        ob = _mlstm(rest, gates, b_conv_w[l].astype(F32), tri_l, tri_u, grp_b, spread, bsz, t)
        oc = _hgrn(rest, log_lb[l][None, :], log_1m_lb[l][None, :], tri_bd, tot_bd, grp_b, bsz, t)
        x2 = _outproj(x2, oa, ob, oc, rest, out_g[l][None, :].astype(F32), grp_a, w_out[l].astype(BF16))
        x2 = _ffn(x2, norm_ffn_g[l][None, :].astype(F32), w_up[l].astype(BF16), ffn_conv_w[l].astype(F32),
                  ffn_conv_b[l][None, :].astype(F32), w_down[l].astype(BF16), t)
    return x2.reshape(bsz, t, d).astype(x.dtype)
```

```python
import functools

import numpy as np
import jax
import jax.numpy as jnp
from jax import lax
from jax.experimental import pallas as pl
from jax.experimental.pallas import tpu as pltpu

F32 = jnp.float32
BF16 = jnp.bfloat16
NEG_INF = float("-inf")
LOG2E = 1.4426950408889634

D_MODEL = 1024
HEAD_DIM = 64
A_HEADS, B_HEADS, C_HEADS = 8, 4, 4
A_DIM, B_DIM, C_DIM = A_HEADS * HEAD_DIM, B_HEADS * HEAD_DIM, C_HEADS * HEAD_DIM
D_MIX = A_DIM + B_DIM + C_DIM
D_FF = 2816
MLSTM_CONV = 4
FFN_CONV = 3
EPS = 1e-6

LANE = 128
BF16_ROWS = 16
VMEM_LIMIT = 56 * 1024 * 1024

QKV_W = 3 * A_DIM
REST_W = 8 * B_DIM
IN_W = QKV_W + REST_W + LANE

TM_PROJ = 512
TM_OUT = 1024
TQ_FOX = 1024
FOX_DIAG_BANDS = 2
FOX_BOUND_MAX = 40.0
FOX_SKIP = 160.0
TC_MLSTM = 512
L_MLSTM = 128
TC_HGRN = 256
L_HGRN = 16
HGRN_SAFE = 60.0
FF_CHUNK = 256


def _reorder_in_cols(w):
    a_f = 3 * A_DIM
    b_qk = a_f + A_HEADS
    b_v = b_qk + 2 * B_DIM
    b_i = b_v + B_DIM
    b_f = b_i + B_HEADS
    b_o = b_f + B_HEADS
    c_q = b_o + B_DIM
    end = c_q + 4 * C_DIM
    sl = lambda lo, hi: lax.slice_in_dim(w, lo, hi, axis=-1)
    n_gate = A_HEADS + 2 * B_HEADS
    pad = jnp.zeros(w.shape[:-1] + (LANE - n_gate,), w.dtype)
    out = jnp.concatenate([sl(0, a_f), sl(b_qk, b_i), sl(b_o, end), sl(a_f, b_qk), sl(b_i, b_o), pad], axis=-1)
    assert out.shape[-1] == IN_W
    return out


def _log_sigmoid(x):
    return jnp.minimum(x, 0.0) - jnp.log(1.0 + jnp.exp(-jnp.abs(x)))


def _sigmoid(x):
    return 1.0 / (1.0 + jnp.exp(-x))


def _head_of_lane(shape):
    return lax.broadcasted_iota(jnp.int32, shape, len(shape) - 1) // HEAD_DIM


def _expand_heads(cols, head):
    out = cols[-1]
    for h in range(len(cols) - 2, -1, -1):
        out = jnp.where(head == h, cols[h], out)
    return out


def _group_mean_sq(z, grp_ref):
    w = z.shape[-1]
    zz = (z * z).astype(BF16)
    return jnp.dot(zz, grp_ref[0:w, 0:w], preferred_element_type=F32) * (1.0 / HEAD_DIM)


def _split3(x):
    hi = x.astype(BF16).astype(F32)
    mid = (x - hi).astype(BF16).astype(F32)
    lo = (x - hi - mid).astype(BF16).astype(F32)
    return hi, mid, lo


def _select_sum(sel, x, parts=3):
    out = None
    for term in _split3(x)[:parts]:
        d = jnp.dot(sel, term.astype(BF16), preferred_element_type=F32)
        out = d if out is None else out + d
    return out


def _sum_select(x, sel, parts=3):
    out = None
    for term in _split3(x)[:parts]:
        d = jnp.dot(term.astype(BF16), sel, preferred_element_type=F32)
        out = d if out is None else out + d
    return out


def _inproj_kernel(x_ref, g_ref, w_ref, b_ref, gq_ref, gk_ref, grp_ref, tri_ref, bound_ref,
                   qa_ref, ka_ref, va_ref, rest_ref, gates_ref, edges_ref, carry_ref, *, tiles_per_seq):
    i = pl.program_id(0)
    tm = x_ref.shape[0]

    @pl.when(i % tiles_per_seq == 0)
    def _():
        carry_ref[...] = jnp.zeros_like(carry_ref)

    x = x_ref[...]
    ms = jnp.mean(x * x, axis=-1, keepdims=True)
    h = (x * lax.rsqrt(ms + EPS) * g_ref[...]).astype(BF16)

    def proj(lo, hi):
        return jnp.dot(h, w_ref[:, lo:hi], preferred_element_type=F32) + b_ref[:, lo:hi]

    gates = proj(QKV_W + REST_W, IN_W)
    gates_ref[...] = gates
    q = proj(0, A_DIM)
    k = proj(A_DIM, 2 * A_DIM)
    v = proj(2 * A_DIM, QKV_W)

    cs = _select_sum(tri_ref[...], _log_sigmoid(gates))
    c2 = (cs + carry_ref[...]) * LOG2E
    carry_ref[...] = carry_ref[...] + cs[tm - 1:tm, :]
    edge_row = lax.broadcasted_iota(jnp.int32, (8, LANE), 0)
    edges_ref[0] = jnp.where(edge_row == 0, c2[0:1, :], jnp.where(edge_row == 1, c2[tm - 1:tm, :], 0.0))

    def head_rsqrt(z):
        ss = jnp.dot((z * z).astype(BF16), grp_ref[...], preferred_element_type=F32) * (1.0 / HEAD_DIM)
        return lax.rsqrt(ss + EPS)

    q_rs, k_rs = head_rsqrt(q), head_rsqrt(k)
    q = q * gq_ref[...]
    k = k * gk_ref[...]
    for c in range(REST_W // A_DIM):
        rest_ref[:, c * A_DIM:(c + 1) * A_DIM] = proj(QKV_W + c * A_DIM, QKV_W + (c + 1) * A_DIM).astype(BF16)

    lane = lax.broadcasted_iota(jnp.int32, (tm, LANE), 1)
    lo_slot = (lane >= HEAD_DIM) & (lane < HEAD_DIM + 3)
    hi_slot = (lane >= HEAD_DIM + 3) & (lane < HEAD_DIM + 6)
    last = lane == HEAD_DIM + 6
    base_q = jnp.where(lo_slot, -1.0, jnp.where(last, 1.0, 0.0))
    base_k = jnp.where(hi_slot, 1.0, jnp.where(last, -bound_ref[...], 0.0))
    ones_v = jnp.where(lane == HEAD_DIM, 1.0, 0.0)
    data = lane < HEAD_DIM
    term = lane % 3
    assert HEAD_DIM % 3 == 1
    for hd in range(A_HEADS):
        pair = slice((hd // 2) * LANE, (hd // 2 + 1) * LANE)
        out = slice(hd * LANE, (hd + 1) * LANE)
        place = (lambda z: z) if hd % 2 == 0 else (lambda z: pltpu.roll(z, HEAD_DIM, 1))
        c_hi, c_mid, c_lo = _split3(jnp.broadcast_to(c2[:, hd:hd + 1], (tm, LANE)))
        terms = jnp.where(term == 1, c_hi, jnp.where(term == 2, c_mid, c_lo))
        aug_q = jnp.where(hi_slot, terms, base_q)
        aug_k = jnp.where(lo_slot, terms, base_k)
        q_scale = jnp.broadcast_to(q_rs[:, hd:hd + 1], (tm, LANE))
        k_scale = jnp.broadcast_to(k_rs[:, hd:hd + 1], (tm, LANE))
        qa_ref[:, out] = jnp.where(data, place(q[:, pair]) * q_scale, aug_q).astype(BF16)
        ka_ref[:, out] = jnp.where(data, place(k[:, pair]) * k_scale, aug_k).astype(BF16)
        va_ref[:, out] = jnp.where(data, place(v[:, pair]), ones_v).astype(BF16)


def _inproj(x2, g, w, b, gq, gk, grp, tri, bound, t):
    n = x2.shape[0]
    tm = TM_PROJ
    aug_w = A_HEADS * LANE
    const = lambda i: (0, 0)
    row = lambda width: pl.BlockSpec((tm, width), lambda i: (i, 0))
    return pl.pallas_call(
        functools.partial(_inproj_kernel, tiles_per_seq=t // tm),
        grid=(n // tm,),
        in_specs=[
            row(D_MODEL),
            pl.BlockSpec((1, D_MODEL), const),
            pl.BlockSpec((D_MODEL, IN_W), const),
            pl.BlockSpec((1, IN_W), const),
            pl.BlockSpec((1, A_DIM), const),
            pl.BlockSpec((1, A_DIM), const),
            pl.BlockSpec((A_DIM, LANE), const),
            pl.BlockSpec((tm, tm), const),
            pl.BlockSpec((1, LANE), const),
        ],
        out_specs=[row(aug_w), row(aug_w), row(aug_w), row(REST_W), row(LANE),
                   pl.BlockSpec((1, 8, LANE), lambda i: (i, 0, 0))],
        out_shape=[
            jax.ShapeDtypeStruct((n, aug_w), BF16),
            jax.ShapeDtypeStruct((n, aug_w), BF16),
            jax.ShapeDtypeStruct((n, aug_w), BF16),
            jax.ShapeDtypeStruct((n, REST_W), BF16),
            jax.ShapeDtypeStruct((n, LANE), F32),
            jax.ShapeDtypeStruct((n // tm, 8, LANE), F32),
        ],
        scratch_shapes=[pltpu.VMEM((1, LANE), F32)],
        compiler_params=pltpu.CompilerParams(dimension_semantics=("arbitrary",), vmem_limit_bytes=VMEM_LIMIT),
        name="inproj",
    )(x2, g, w, b, gq, gk, grp, tri, bound)


def _fox_kernel(first_ref, q_ref, k_ref, v_ref, o_ref, *, tq):
    i = pl.program_id(2)
    step = (pl.program_id(0) * pl.num_programs(1) + pl.program_id(1)) * pl.num_programs(2) + i
    nt = (((1,), (1,)), ((), ()))
    half = tq // 2

    def attend(q, keys, hh, visible=None):
        cols = slice(hh * LANE, (hh + 1) * LANE)
        s = lax.dot_general(q, k_ref[keys, cols], nt, preferred_element_type=F32)
        if visible is not None:
            s = jnp.where(visible, s, NEG_INF)
        return jnp.dot(jnp.exp2(s).astype(BF16), v_ref[keys, cols], preferred_element_type=F32)

    def full_step(j, accs):
        keys = pl.ds(pl.multiple_of(j * tq, tq), tq)
        return tuple(accs[hh] + attend(q_ref[:, hh * LANE:(hh + 1) * LANE], keys, hh) for hh in range(2))

    dead_halves = first_ref[step]

    def half_step():
        keys = pl.ds(pl.multiple_of(dead_halves * half, half), half)
        return tuple(attend(q_ref[:, hh * LANE:(hh + 1) * LANE], keys, hh) for hh in range(2))

    zero = jnp.zeros((tq, LANE), F32)
    accs = lax.cond(dead_halves % 2 == 1, half_step, lambda: (zero, zero))
    accs = lax.fori_loop((dead_halves + 1) // 2, i, full_step, accs)

    off = pl.multiple_of(i * tq, tq)
    band = tq // FOX_DIAG_BANDS
    lane = lax.broadcasted_iota(jnp.int32, (band, LANE), 1)
    for r in range(FOX_DIAG_BANDS):
        rows = slice(r * band, (r + 1) * band)
        n_keys = (r + 1) * band
        visible = (lax.broadcasted_iota(jnp.int32, (band, n_keys), 1)
                   <= lax.broadcasted_iota(jnp.int32, (band, n_keys), 0) + r * band)
        outs = []
        for hh in range(2):
            cols = slice(hh * LANE, (hh + 1) * LANE)
            acc = accs[hh][rows] + attend(q_ref[rows, cols], pl.ds(off, n_keys), hh, visible)
            outs.append(acc / acc[:, HEAD_DIM:HEAD_DIM + 1])
        o_ref[rows, :] = jnp.where(lane < HEAD_DIM, outs[0], pltpu.roll(outs[1], HEAD_DIM, 1)).astype(BF16)


def _fox(qa, ka, va, c_edges, bsz, t):
    n = qa.shape[0]
    tq = TQ_FOX
    nq = t // tq
    pairs = A_HEADS // 2

    assert tq == 2 * TM_PROJ
    edges = c_edges.reshape(bsz, 2 * nq, 8, LANE)
    c_first = edges[:, 0::2, 0, :A_HEADS]
    c_last = edges[:, :, 1, :A_HEADS]
    gap = c_first[:, :, None, :] - c_last[:, None, :, :]
    dead = jnp.all((gap < -FOX_SKIP).reshape(bsz, nq, 2 * nq, pairs, 2), axis=-1)
    dead = dead & (2 * jnp.arange(nq)[None, :, None, None] > jnp.arange(2 * nq)[None, None, :, None])
    first = jnp.sum(dead, axis=2).astype(jnp.int32).transpose(0, 2, 1).reshape(-1)

    return pl.pallas_call(
        functools.partial(_fox_kernel, tq=tq),
        grid_spec=pltpu.PrefetchScalarGridSpec(
            num_scalar_prefetch=1,
            grid=(bsz, pairs, nq),
            in_specs=[
                pl.BlockSpec((tq, 2 * LANE), lambda b, p, i, first: (b * nq + i, p)),
                pl.BlockSpec((t, 2 * LANE), lambda b, p, i, first: (b, p)),
                pl.BlockSpec((t, 2 * LANE), lambda b, p, i, first: (b, p)),
            ],
            out_specs=pl.BlockSpec((tq, LANE), lambda b, p, i, first: (b * nq + i, p)),
        ),
        out_shape=jax.ShapeDtypeStruct((n, A_DIM), BF16),
        compiler_params=pltpu.CompilerParams(
            dimension_semantics=("arbitrary", "arbitrary", "arbitrary"), vmem_limit_bytes=VMEM_LIMIT),
        name="fox",
    )(first, qa, ka, va)


def _fox_safe_kernel(q_ref, k_ref, v_ref, c_ref, o_ref, *, tq):
    i = pl.program_id(2)
    q = q_ref[...]
    lane = lax.broadcasted_iota(jnp.int32, (tq, LANE), 1)
    zero = jnp.zeros_like(q)
    qh = (jnp.where(lane < HEAD_DIM, q, zero), jnp.where(lane >= HEAD_DIM, q, zero))
    tri = lax.broadcasted_iota(jnp.int32, (tq, tq), 1) <= lax.broadcasted_iota(jnp.int32, (tq, tq), 0)

    def step(j, carry, masked):
        off = pl.multiple_of(j * tq, tq)
        kt = k_ref[pl.ds(off, tq), :]
        vt = v_ref[pl.ds(off, tq), :]
        new = []
        for hh in range(2):
            m, l, acc = carry[hh]
            s = lax.dot_general(qh[hh], kt, (((1,), (1,)), ((), ())), preferred_element_type=F32)
            s = s - c_ref[0, hh:hh + 1, pl.ds(off, tq)]
            if masked:
                s = jnp.where(tri, s, NEG_INF)
            m_new = jnp.maximum(m, jnp.max(s, axis=-1, keepdims=True))
            alpha = jnp.exp2(m - m_new)
            p = jnp.exp2(s - m_new)
            l = alpha * l + jnp.sum(p, axis=-1, keepdims=True)
            acc = alpha * acc + jnp.dot(p.astype(BF16), vt, preferred_element_type=F32)
            new.append((m_new, l, acc))
        return tuple(new)

    init = tuple((jnp.full((tq, 1), NEG_INF, F32), jnp.zeros((tq, 1), F32), jnp.zeros((tq, LANE), F32))
                 for _ in range(2))
    carry = lax.fori_loop(0, i, lambda j, c: step(j, c, False), init)
    (_, l0, a0), (_, l1, a1) = step(i, carry, True)
    o_ref[...] = jnp.where(lane < HEAD_DIM, a0 / l0, a1 / l1).astype(BF16)


def _fox_safe(qa, ka, va, c_edges, bsz, t):
    del c_edges
    n = qa.shape[0]
    tq = TQ_FOX // 2
    nq = t // tq
    pairs = A_HEADS // 2
    heads = lambda z: z.reshape(n, A_HEADS, LANE)
    unpack = lambda z: heads(z)[:, :, :HEAD_DIM].reshape(n, A_DIM)
    c = jnp.sum(heads(ka)[:, :, HEAD_DIM:HEAD_DIM + 3].astype(F32), axis=-1)
    c_t = c.reshape(bsz, t, A_HEADS).transpose(0, 2, 1).reshape(bsz * pairs, 2, t)
    return pl.pallas_call(
        functools.partial(_fox_safe_kernel, tq=tq),
        grid=(bsz, pairs, nq),
        in_specs=[
            pl.BlockSpec((tq, LANE), lambda b, p, i: (b * nq + i, p)),
            pl.BlockSpec((t, LANE), lambda b, p, i: (b, p)),
            pl.BlockSpec((t, LANE), lambda b, p, i: (b, p)),
            pl.BlockSpec((1, 2, t), lambda b, p, i: (b * pairs + p, 0, 0)),
        ],
        out_specs=pl.BlockSpec((tq, LANE), lambda b, p, i: (b * nq + i, p)),
        out_shape=jax.ShapeDtypeStruct((n, A_DIM), BF16),
        compiler_params=pltpu.CompilerParams(
            dimension_semantics=("arbitrary", "arbitrary", "arbitrary"), vmem_limit_bytes=VMEM_LIMIT),
        name="fox_safe",
    )(unpack(qa), unpack(ka), unpack(va), c_t)


def _mlstm_kernel(qk_ref, halo_ref, v_ref, gates_ref, cw_ref, tri_ref, triu_ref, grp_ref, spread_ref, o_ref,
                  buf_ref, c_st, n_st, m_st, *, tc, chunk):
    j = pl.program_id(1)
    width = B_DIM

    @pl.when(j == 0)
    def _():
        c_st[...] = jnp.zeros_like(c_st)
        n_st[...] = jnp.zeros_like(n_st)
        m_st[...] = jnp.full(m_st.shape, NEG_INF, F32)

    hr = BF16_ROWS
    buf_ref[0:hr, :] = jnp.where(j == 0, 0.0, halo_ref[...].astype(F32))
    buf_ref[hr:hr + tc, :] = qk_ref[...].astype(F32)
    cw = cw_ref[...]
    y = cw[MLSTM_CONV - 1:MLSTM_CONV] * buf_ref[hr:hr + tc, :]
    for d in range(1, MLSTM_CONV):
        y = y + cw[MLSTM_CONV - 1 - d:MLSTM_CONV - d] * buf_ref[hr - d:hr - d + tc, :]
    y = y * _sigmoid(y)
    q_all = y[:, :width]
    k_all = y[:, width:] * (HEAD_DIM ** -0.5)

    head = _head_of_lane((chunk, width))
    head_sq = _head_of_lane((width, width))
    blockdiag = head_sq == lax.broadcasted_iota(jnp.int32, (width, width), 0) // HEAD_DIM
    causal = lax.broadcasted_iota(jnp.int32, (chunk, chunk), 1) <= lax.broadcasted_iota(jnp.int32, (chunk, chunk), 0)
    tri = tri_ref[...]
    grp = grp_ref[...]

    gx = _sum_select(gates_ref[...], spread_ref[...])
    i_all = gx[:, :width]
    f_all = gx[:, width:]

    local = []
    for c in range(tc // chunk):
        r0 = c * chunk
        q = q_all[r0:r0 + chunk]
        k = k_all[r0:r0 + chunk]
        vb = v_ref[r0:r0 + chunk, :]
        qb, kb = q.astype(BF16), k.astype(BF16)
        bc_x = _select_sum(tri, _log_sigmoid(f_all[r0:r0 + chunk]))
        g_t = gates_ref[r0:r0 + chunk, :].T[A_HEADS:A_HEADS + 2 * B_HEADS, :]
        bc_t = _sum_select(_log_sigmoid(g_t), triu_ref[...])

        m_cols, den_cols = [], []
        num_x = jnp.zeros((chunk, width), F32)
        for h in range(B_HEADS):
            lo = h * HEAD_DIM
            row = g_t[h:h + 1, :] - bc_t[B_HEADS + h:B_HEADS + h + 1, :]
            log_d = jnp.where(causal, bc_x[:, lo:lo + 1] + row, NEG_INF)
            m_loc = jnp.max(log_d, axis=-1, keepdims=True)
            s = lax.dot_general(jnp.where(head == h, qb, jnp.zeros_like(qb)), kb, (((1,), (1,)), ((), ())),
                                preferred_element_type=F32)
            sqk = s * jnp.exp(log_d - m_loc)
            num_x = jnp.where(head == h, jnp.dot(sqk.astype(BF16), vb, preferred_element_type=F32), num_x)
            m_cols.append(m_loc)
            den_cols.append(jnp.sum(sqk, axis=-1, keepdims=True))

        g_x = bc_x[chunk - 1:chunk, :]
        a_x = g_x - bc_x + i_all[r0:r0 + chunk]
        a_max = jnp.max(a_x, axis=0, keepdims=True)
        kw = k * jnp.exp(a_x - a_max)
        upd = lax.dot_general(kw.astype(BF16), vb, (((0,), (0,)), ((), ())), preferred_element_type=F32)
        local.append((q, qb, bc_x, _expand_heads(m_cols, head), num_x, _expand_heads(den_cols, head), g_x, a_max,
                      jnp.where(blockdiag, upd, 0.0), jnp.sum(kw, axis=0, keepdims=True)))

    for c, (q, qb, bc_x, m_loc_x, num_x, den_x, g_x, a_max, upd, k_sum) in enumerate(local):
        r0 = c * chunk
        c_prev, n_prev, m_prev = c_st[...], n_st[...], m_st[...]
        m_inter = bc_x + m_prev
        m_out_x = jnp.maximum(m_inter, m_loc_x)
        intra_w = jnp.exp(m_loc_x - m_out_x)
        inter_w = jnp.exp(m_inter - m_out_x)
        q_c = jnp.dot(qb, c_prev.astype(BF16), preferred_element_type=F32)
        q_n = _sum_select(q * n_prev, grp, parts=2)
        num = intra_w * num_x + inter_w * q_c
        den = intra_w * den_x + inter_w * q_n
        o_ref[r0:r0 + chunk, :] = (num / jnp.maximum(jnp.abs(den), jnp.exp(-m_out_x))).astype(BF16)

        m_new = jnp.maximum(g_x + m_prev, a_max)
        decay = jnp.exp(g_x + m_prev - m_new)
        fresh = jnp.exp(a_max - m_new)
        c_st[...] = decay * c_prev + fresh * upd
        n_st[...] = decay * n_prev + fresh * k_sum
        m_st[...] = m_new


def _mlstm(rest, gates, conv_w, tri_l, tri_u, grp, spread, bsz, t):
    n = rest.shape[0]
    tc, nt = TC_MLSTM, t // TC_MLSTM
    hr = BF16_ROWS
    const = lambda b, j: (0, 0)
    return pl.pallas_call(
        functools.partial(_mlstm_kernel, tc=tc, chunk=L_MLSTM),
        grid=(bsz, nt),
        in_specs=[
            pl.BlockSpec((tc, 2 * B_DIM), lambda b, j: (b * nt + j, 0)),
            pl.BlockSpec((hr, 2 * B_DIM), lambda b, j: (jnp.maximum((b * nt + j) * (tc // hr) - 1, 0), 0)),
            pl.BlockSpec((tc, B_DIM), lambda b, j: (b * nt + j, 2)),
            pl.BlockSpec((tc, LANE), lambda b, j: (b * nt + j, 0)),
            pl.BlockSpec((MLSTM_CONV, 2 * B_DIM), const),
            pl.BlockSpec((L_MLSTM, L_MLSTM), const),
            pl.BlockSpec((L_MLSTM, L_MLSTM), const),
            pl.BlockSpec((B_DIM, B_DIM), const),
            pl.BlockSpec((LANE, 2 * B_DIM), const),
        ],
        out_specs=pl.BlockSpec((tc, B_DIM), lambda b, j: (b * nt + j, 0)),
        out_shape=jax.ShapeDtypeStruct((n, B_DIM), BF16),
        scratch_shapes=[
            pltpu.VMEM((tc + hr, 2 * B_DIM), F32),
            pltpu.VMEM((B_DIM, B_DIM), F32),
            pltpu.VMEM((1, B_DIM), F32),
            pltpu.VMEM((1, B_DIM), F32),
        ],
        compiler_params=pltpu.CompilerParams(dimension_semantics=("arbitrary", "arbitrary"),
                                             vmem_limit_bytes=VMEM_LIMIT),
        name="mlstm",
    )(rest, rest, rest, gates, conv_w, tri_l, tri_u, grp, spread)


def _hgrn_kernel(q_ref, f_ref, i_ref, la_ref, l1m_ref, tri_ref, tot_ref, grp_ref, o_ref, low_ref, st_ref,
                 *, tc, chunk, direct):
    nb = q_ref.shape[0]
    width = C_DIM

    @pl.when(pl.program_id(0) == 0)
    def _():
        st_ref[...] = jnp.zeros_like(st_ref)

    head = _head_of_lane((tc, width))
    head_sq = _head_of_lane((width, width))
    blockdiag = head_sq == lax.broadcasted_iota(jnp.int32, (width, width), 0) // HEAD_DIM
    rr = lax.broadcasted_iota(jnp.int32, (tc, tc), 0)
    cc = lax.broadcasted_iota(jnp.int32, (tc, tc), 1)
    keep = (cc <= rr) & (cc // chunk == rr // chunk)

    seqs = []
    low = None
    for s in range(nb):
        xq = q_ref[s].astype(F32)
        q = xq * _sigmoid(xq)
        xf = f_ref[s].astype(F32)
        ls = _log_sigmoid(xf)
        a = la_ref[...]
        b = l1m_ref[...] + ls
        logf = jnp.maximum(a, b) + jnp.log(1.0 + jnp.exp(-jnp.abs(a - b)))
        kk = jnp.exp(l1m_ref[...] + ls - xf)
        vb = i_ref[s]

        bc = _select_sum(tri_ref[...], logf)
        blast = _select_sum(tot_ref[...], logf)
        qt = (q * jnp.exp(bc)).astype(BF16)
        kd = (kk * jnp.exp(blast - bc)).astype(BF16)

        def factored_intra(q=q, kk=kk, bc=bc, qt=qt, vb=vb):
            kt = (kk * jnp.exp(-bc)).astype(BF16)
            intra = jnp.zeros((tc, width), F32)
            for h in range(C_HEADS):
                att = lax.dot_general(jnp.where(head == h, qt, jnp.zeros_like(qt)), kt, (((1,), (1,)), ((), ())),
                                      preferred_element_type=F32)
                att = jnp.where(keep, att, 0.0).astype(BF16)
                intra = jnp.where(head == h, jnp.dot(att, vb, preferred_element_type=F32), intra)
            return intra

        def direct_intra(q=q, kk=kk, bc=bc, vb=vb):
            v = vb.astype(F32)
            pos = lax.broadcasted_iota(jnp.int32, (tc, width), 0) % chunk
            intra = jnp.zeros((tc, width), F32)
            for off in range(chunk):
                back = (lambda z: z) if off == 0 else (lambda z: pltpu.roll(z, off, 0))
                w = jnp.where(pos >= off, q * back(kk) * jnp.exp(jnp.minimum(bc - back(bc), 0.0)), 0.0)
                intra = intra + _sum_select(w, grp_ref[...]) * back(v)
            return intra

        intra = direct_intra() if direct else factored_intra()
        seqs.append((qt, kd, vb, jnp.exp(blast), intra))
        bc_min = jnp.min(bc, axis=(0, 1), keepdims=True)
        low = bc_min if low is None else jnp.minimum(low, bc_min)
    low_ref[0] = jnp.broadcast_to(low, low_ref.shape[1:])

    for c in range(tc // chunk):
        r0 = c * chunk
        for s, (qt, kd, vb, chunk_decay, intra) in enumerate(seqs):
            s_t = st_ref[s]
            inter = lax.dot_general(qt[r0:r0 + chunk], s_t.astype(BF16), (((1,), (1,)), ((), ())),
                                    preferred_element_type=F32)
            o_ref[s, r0:r0 + chunk, :] = (intra[r0:r0 + chunk] + inter).astype(BF16)
            upd = lax.dot_general(vb[r0:r0 + chunk], kd[r0:r0 + chunk], (((0,), (0,)), ((), ())),
                                  preferred_element_type=F32)
            st_ref[s] = s_t * chunk_decay[r0:r0 + 1, :] + jnp.where(blockdiag, upd, 0.0)


def _hgrn(rest, log_lb, log_1m_lb, tri_bd, tot_bd, grp, bsz, t):
    fast, low = _hgrn_call(rest, log_lb, log_1m_lb, tri_bd, tot_bd, grp, bsz, t, direct=False)
    redo = lambda: _hgrn_call(rest, log_lb, log_1m_lb, tri_bd, tot_bd, grp, bsz, t, direct=True)[0]
    return lax.cond(jnp.min(low) < -HGRN_SAFE, redo, lambda: fast)


def _hgrn_call(rest, log_lb, log_1m_lb, tri_bd, tot_bd, grp, bsz, t, direct):
    tc, nt = TC_HGRN, t // TC_HGRN
    rest3 = rest.reshape(bsz, t, REST_W)
    blk = lambda col: pl.BlockSpec((bsz, tc, C_DIM), lambda j: (0, j, col))
    const = lambda j: (0, 0)
    out, low = pl.pallas_call(
        functools.partial(_hgrn_kernel, tc=tc, chunk=L_HGRN, direct=direct),
        grid=(nt,),
        in_specs=[
            blk(4), blk(5), blk(6),
            pl.BlockSpec((1, C_DIM), const),
            pl.BlockSpec((1, C_DIM), const),
            pl.BlockSpec((tc, tc), const),
            pl.BlockSpec((tc, tc), const),
            pl.BlockSpec((C_DIM, C_DIM), const),
        ],
        out_specs=[pl.BlockSpec((bsz, tc, C_DIM), lambda j: (0, j, 0)),
                   pl.BlockSpec((1, 8, LANE), lambda j: (j, 0, 0))],
        out_shape=[jax.ShapeDtypeStruct((bsz, t, C_DIM), BF16), jax.ShapeDtypeStruct((nt, 8, LANE), F32)],
        scratch_shapes=[pltpu.VMEM((bsz, C_DIM, C_DIM), F32)],
        compiler_params=pltpu.CompilerParams(dimension_semantics=("arbitrary",), vmem_limit_bytes=VMEM_LIMIT),
        name="hgrn2_direct" if direct else "hgrn2",
    )(rest3, rest3, rest3, log_lb, log_1m_lb, tri_bd, tot_bd, grp)
    return out.reshape(bsz * t, C_DIM), low


def _outproj_kernel(x_ref, oa_ref, ob_ref, oc_ref, bo_ref, cg_ref, g_ref, grp_ref, w_ref, o_ref):
    def normed(o, lo, hi):
        return o * lax.rsqrt(_group_mean_sq(o, grp_ref) + EPS) * g_ref[:, lo:hi]

    ya = normed(oa_ref[...].astype(F32), 0, A_DIM)
    yb = _sigmoid(bo_ref[...].astype(F32)) * normed(ob_ref[...].astype(F32), A_DIM, A_DIM + B_DIM)
    cg = cg_ref[...].astype(F32)
    yc = cg * _sigmoid(cg) * normed(oc_ref[...].astype(F32), A_DIM + B_DIM, D_MIX)
    acc = x_ref[...] + jnp.dot(ya.astype(BF16), w_ref[0:A_DIM, :], preferred_element_type=F32)
    acc = acc + jnp.dot(yb.astype(BF16), w_ref[A_DIM:A_DIM + B_DIM, :], preferred_element_type=F32)
    acc = acc + jnp.dot(yc.astype(BF16), w_ref[A_DIM + B_DIM:D_MIX, :], preferred_element_type=F32)
    o_ref[...] = acc


def _outproj(x2, oa, ob, oc, rest, out_g, grp, w_out):
    n = x2.shape[0]
    tm = TM_OUT
    const = lambda i: (0, 0)
    return pl.pallas_call(
        _outproj_kernel,
        grid=(n // tm,),
        in_specs=[
            pl.BlockSpec((tm, D_MODEL), lambda i: (i, 0)),
            pl.BlockSpec((tm, A_DIM), lambda i: (i, 0)),
            pl.BlockSpec((tm, B_DIM), lambda i: (i, 0)),
            pl.BlockSpec((tm, C_DIM), lambda i: (i, 0)),
            pl.BlockSpec((tm, B_DIM), lambda i: (i, 3)),
            pl.BlockSpec((tm, C_DIM), lambda i: (i, 7)),
            pl.BlockSpec((1, D_MIX), const),
            pl.BlockSpec((A_DIM, A_DIM), const),
            pl.BlockSpec((D_MIX, D_MODEL), const),
        ],
        out_specs=pl.BlockSpec((tm, D_MODEL), lambda i: (i, 0)),
        out_shape=jax.ShapeDtypeStruct((n, D_MODEL), F32),
        compiler_params=pltpu.CompilerParams(dimension_semantics=("arbitrary",), vmem_limit_bytes=VMEM_LIMIT),
        name="outproj",
    )(x2, oa, ob, oc, rest, rest, out_g, grp, w_out)


def _ffn_kernel(x_ref, halo_ref, g_ref, wup_ref, cw_ref, cb_ref, wdn_ref, o_ref, h_ref, ug_ref, uu_ref, act_ref,
                *, tm, tiles_per_seq):
    i = pl.program_id(0)
    halo_rows = BF16_ROWS

    def rms(x):
        ms = jnp.mean(x * x, axis=-1, keepdims=True)
        return x * lax.rsqrt(ms + EPS) * g_ref[...]

    x = x_ref[...]
    first = (i % tiles_per_seq) == 0
    h_ref[0:halo_rows, :] = jnp.where(first, 0.0, rms(halo_ref[...])).astype(BF16)
    h_ref[halo_rows:, :] = rms(x).astype(BF16)
    h = h_ref[...]

    def conv(u_ref, slot, lo):
        y = cb_ref[:, lo:lo + FF_CHUNK] + cw_ref[FFN_CONV - 1:FFN_CONV, lo:lo + FF_CHUNK] * u_ref[slot, halo_rows:, :]
        for d in range(1, FFN_CONV):
            y = y + cw_ref[FFN_CONV - 1 - d:FFN_CONV - d, lo:lo + FF_CHUNK] * u_ref[slot, pl.ds(halo_rows - d, tm), :]
        return y

    def up_proj(c):
        lo = c * FF_CHUNK
        ug_ref[c % 2] = jnp.dot(h, wup_ref[:, lo:lo + FF_CHUNK], preferred_element_type=F32)
        uu_ref[c % 2] = jnp.dot(h, wup_ref[:, D_FF + lo:D_FF + lo + FF_CHUNK], preferred_element_type=F32)

    n_chunks = D_FF // FF_CHUNK
    up_proj(0)
    for c in range(n_chunks):
        if c + 1 < n_chunks:
            up_proj(c + 1)
        lo = c * FF_CHUNK
        gate = conv(ug_ref, c % 2, lo)
        up = conv(uu_ref, c % 2, D_FF + lo)
        act_ref[:, lo:lo + FF_CHUNK] = (gate * _sigmoid(gate) * up).astype(BF16)
    o_ref[...] = x + jnp.dot(act_ref[...], wdn_ref[...], preferred_element_type=F32)


def _ffn(x2, g, w_up, conv_w, conv_b, w_down, t):
    n = x2.shape[0]
    tm = TM_PROJ
    halo = BF16_ROWS
    const = lambda i: (0, 0)
    resident = pl.Buffered(1)
    return pl.pallas_call(
        functools.partial(_ffn_kernel, tm=tm, tiles_per_seq=t // tm),
        grid=(n // tm,),
        in_specs=[
            pl.BlockSpec((tm, D_MODEL), lambda i: (i, 0)),
            pl.BlockSpec((halo, D_MODEL), lambda i: (jnp.maximum(i * (tm // halo) - 1, 0), 0)),
            pl.BlockSpec((1, D_MODEL), const),
            pl.BlockSpec((D_MODEL, 2 * D_FF), const, pipeline_mode=resident),
            pl.BlockSpec((FFN_CONV, 2 * D_FF), const),
            pl.BlockSpec((1, 2 * D_FF), const),
            pl.BlockSpec((D_FF, D_MODEL), const, pipeline_mode=resident),
        ],
        out_specs=pl.BlockSpec((tm, D_MODEL), lambda i: (i, 0)),
        out_shape=jax.ShapeDtypeStruct((n, D_MODEL), F32),
        scratch_shapes=[
            pltpu.VMEM((tm + halo, D_MODEL), BF16),
            pltpu.VMEM((2, tm + halo, FF_CHUNK), F32),
            pltpu.VMEM((2, tm + halo, FF_CHUNK), F32),
            pltpu.VMEM((tm, D_FF), BF16),
        ],
        compiler_params=pltpu.CompilerParams(dimension_semantics=("arbitrary",), vmem_limit_bytes=VMEM_LIMIT),
        name="ffn",
    )(x2, x2, g, w_up, conv_w, conv_b, w_down)


def _block_diag_ones(n, blk, dtype):
    r = np.arange(n) // blk
    return jnp.asarray(r[:, None] == r[None, :], dtype)


def kernel(x, lb_logits, norm_mix_g, w_in, b_in, a_q_g, a_k_g, b_conv_w, out_g, w_out, norm_ffn_g, w_up,
           ffn_conv_w, ffn_conv_b, w_down):
    bsz, t, d = x.shape
    depth = w_in.shape[0]
    assert d == D_MODEL and t % TQ_FOX == 0 and t % TC_MLSTM == 0 and t % TC_HGRN == 0
    n = bsz * t
    x2 = x.reshape(n, d).astype(F32)

    p = jax.nn.softmax(lb_logits.astype(F32), axis=0)
    lb_all = jnp.maximum(jnp.cumsum(p, axis=0) - p[0], 0.0)
    log_lb = jnp.log(lb_all)
    log_1m_lb = jnp.log1p(-lb_all)

    grp_a = _block_diag_ones(A_DIM, HEAD_DIM, BF16)
    head_sel = jnp.asarray(np.arange(A_DIM)[:, None] // HEAD_DIM == np.arange(LANE)[None, :], BF16)
    grp_b = _block_diag_ones(B_DIM, HEAD_DIM, BF16)
    ar = np.arange(L_MLSTM)
    tri_u = jnp.asarray(ar[:, None] <= ar[None, :], BF16)
    tri_l = jnp.asarray(ar[:, None] >= ar[None, :], BF16)
    gate_row = np.arange(LANE)[:, None] - A_HEADS
    spread = jnp.asarray(gate_row == np.arange(2 * B_DIM)[None, :] // HEAD_DIM, BF16)
    ap = np.arange(TM_PROJ)
    tri_proj = jnp.asarray(ap[:, None] >= ap[None, :], BF16)
    ah = np.arange(TC_HGRN)
    same = (ah[:, None] // L_HGRN) == (ah[None, :] // L_HGRN)
    tri_bd = jnp.asarray(same & (ah[:, None] >= ah[None, :]), BF16)
    tot_bd = jnp.asarray(same, BF16)

    w_in_r = _reorder_in_cols(w_in).astype(BF16)
    for l in range(depth):
        w_r = w_in_r[l]
        b_r = _reorder_in_cols(b_in[l].astype(F32))[None, :]
        gq = (jnp.tile(a_q_g[l].astype(F32), A_HEADS) * (HEAD_DIM ** -0.5 * LOG2E))[None, :]
        gk = jnp.tile(a_k_g[l].astype(F32), A_HEADS)[None, :]
        bound = HEAD_DIM * jnp.max(jnp.abs(gq)) * jnp.max(jnp.abs(gk))
        qa, ka, va, rest, gates, c_edges = _inproj(x2, norm_mix_g[l][None, :].astype(F32), w_r, b_r, gq, gk, head_sel, tri_proj,
                                          jnp.full((1, LANE), bound, F32), t)
        oa = lax.cond(bound < FOX_BOUND_MAX, functools.partial(_fox, bsz=bsz, t=t),
                      functools.partial(_fox_safe, bsz=bsz, t=t), qa, ka, va, c_edges)
        ob = _mlstm(rest, gates, b_conv_w[l].astype(F32), tri_l, tri_u, grp_b, spread, bsz, t)
        oc = _hgrn(rest, log_lb[l][None, :], log_1m_lb[l][None, :], tri_bd, tot_bd, grp_b, bsz, t)
        x2 = _outproj(x2, oa, ob, oc, rest, out_g[l][None, :].astype(F32), grp_a, w_out[l].astype(BF16))
        x2 = _ffn(x2, norm_ffn_g[l][None, :].astype(F32), w_up[l].astype(BF16), ffn_conv_w[l].astype(F32),
                  ffn_conv_b[l][None, :].astype(F32), w_down[l].astype(BF16), t)
    return x2.reshape(bsz, t, d).astype(x.dtype)
```

```python
import functools

import numpy as np
import jax
import jax.numpy as jnp
from jax import lax
from jax.experimental import pallas as pl
from jax.experimental.pallas import tpu as pltpu

F32 = jnp.float32
BF16 = jnp.bfloat16
NEG_INF = float("-inf")
LOG2E = 1.4426950408889634

D_MODEL = 1024
HEAD_DIM = 64
A_HEADS, B_HEADS, C_HEADS = 8, 4, 4
A_DIM, B_DIM, C_DIM = A_HEADS * HEAD_DIM, B_HEADS * HEAD_DIM, C_HEADS * HEAD_DIM
D_MIX = A_DIM + B_DIM + C_DIM
D_FF = 2816
MLSTM_CONV = 4
FFN_CONV = 3
EPS = 1e-6

LANE = 128
BF16_ROWS = 16
VMEM_LIMIT = 56 * 1024 * 1024

QKV_W = 3 * A_DIM
REST_W = 8 * B_DIM
IN_W = QKV_W + REST_W + LANE

TM_PROJ = 512
TM_OUT = 1024
TQ_FOX = 1024
FOX_DIAG_BANDS = 2
FOX_BOUND_MAX = 40.0
FOX_SKIP = 160.0
TC_MLSTM = 512
L_MLSTM = 128
TC_HGRN = 256
L_HGRN = 16
HGRN_SAFE = 60.0
FF_CHUNK = 256
FF_SLOTS = 2


def _reorder_in_cols(w):
    a_f = 3 * A_DIM
    b_qk = a_f + A_HEADS
    b_v = b_qk + 2 * B_DIM
    b_i = b_v + B_DIM
    b_f = b_i + B_HEADS
    b_o = b_f + B_HEADS
    c_q = b_o + B_DIM
    end = c_q + 4 * C_DIM
    sl = lambda lo, hi: lax.slice_in_dim(w, lo, hi, axis=-1)
    n_gate = A_HEADS + 2 * B_HEADS
    pad = jnp.zeros(w.shape[:-1] + (LANE - n_gate,), w.dtype)
    out = jnp.concatenate([sl(0, a_f), sl(b_qk, b_i), sl(b_o, end), sl(a_f, b_qk), sl(b_i, b_o), pad], axis=-1)
    assert out.shape[-1] == IN_W
    return out


def _log_sigmoid(x):
    return jnp.minimum(x, 0.0) - jnp.log(1.0 + jnp.exp(-jnp.abs(x)))


def _sigmoid(x):
    return 1.0 / (1.0 + jnp.exp(-x))


def _head_of_lane(shape):
    return lax.broadcasted_iota(jnp.int32, shape, len(shape) - 1) // HEAD_DIM


def _expand_heads(cols, head):
    out = cols[-1]
    for h in range(len(cols) - 2, -1, -1):
        out = jnp.where(head == h, cols[h], out)
    return out


def _group_mean_sq(z, grp_ref):
    w = z.shape[-1]
    zz = (z * z).astype(BF16)
    return jnp.dot(zz, grp_ref[0:w, 0:w], preferred_element_type=F32) * (1.0 / HEAD_DIM)


def _split3(x):
    hi = x.astype(BF16).astype(F32)
    mid = (x - hi).astype(BF16).astype(F32)
    lo = (x - hi - mid).astype(BF16).astype(F32)
    return hi, mid, lo


def _select_sum(sel, x, parts=3):
    out = None
    for term in _split3(x)[:parts]:
        d = jnp.dot(sel, term.astype(BF16), preferred_element_type=F32)
        out = d if out is None else out + d
    return out


def _sum_select(x, sel, parts=3):
    out = None
    for term in _split3(x)[:parts]:
        d = jnp.dot(term.astype(BF16), sel, preferred_element_type=F32)
        out = d if out is None else out + d
    return out


def _inproj_kernel(x_ref, g_ref, w_ref, b_ref, gq_ref, gk_ref, grp_ref, tri_ref, bound_ref,
                   qa_ref, ka_ref, va_ref, rest_ref, gates_ref, edges_ref, carry_ref, *, tiles_per_seq):
    i = pl.program_id(0)
    tm = x_ref.shape[0]

    @pl.when(i % tiles_per_seq == 0)
    def _():
        carry_ref[...] = jnp.zeros_like(carry_ref)

    x = x_ref[...]
    ms = jnp.mean(x * x, axis=-1, keepdims=True)
    h = (x * lax.rsqrt(ms + EPS) * g_ref[...]).astype(BF16)

    def proj(lo, hi):
        return jnp.dot(h, w_ref[:, lo:hi], preferred_element_type=F32) + b_ref[:, lo:hi]

    gates = proj(QKV_W + REST_W, IN_W)
    gates_ref[...] = gates
    q = proj(0, A_DIM)
    k = proj(A_DIM, 2 * A_DIM)
    v = proj(2 * A_DIM, QKV_W)

    cs = _select_sum(tri_ref[...], _log_sigmoid(gates))
    c2 = (cs + carry_ref[...]) * LOG2E
    carry_ref[...] = carry_ref[...] + cs[tm - 1:tm, :]
    edge_row = lax.broadcasted_iota(jnp.int32, (8, LANE), 0)
    edges_ref[0] = jnp.where(edge_row == 0, c2[0:1, :], jnp.where(edge_row == 1, c2[tm - 1:tm, :], 0.0))

    def head_rsqrt(z):
        ss = jnp.dot((z * z).astype(BF16), grp_ref[...], preferred_element_type=F32) * (1.0 / HEAD_DIM)
        return lax.rsqrt(ss + EPS)

    q_rs, k_rs = head_rsqrt(q), head_rsqrt(k)
    q = q * gq_ref[...]
    k = k * gk_ref[...]
    for c in range(REST_W // A_DIM):
        rest_ref[:, c * A_DIM:(c + 1) * A_DIM] = proj(QKV_W + c * A_DIM, QKV_W + (c + 1) * A_DIM).astype(BF16)

    lane = lax.broadcasted_iota(jnp.int32, (tm, LANE), 1)
    lo_slot = (lane >= HEAD_DIM) & (lane < HEAD_DIM + 3)
    hi_slot = (lane >= HEAD_DIM + 3) & (lane < HEAD_DIM + 6)
    last = lane == HEAD_DIM + 6
    base_q = jnp.where(lo_slot, -1.0, jnp.where(last, 1.0, 0.0))
    base_k = jnp.where(hi_slot, 1.0, jnp.where(last, -bound_ref[...], 0.0))
    ones_v = jnp.where(lane == HEAD_DIM, 1.0, 0.0)
    data = lane < HEAD_DIM
    term = lane % 3
    assert HEAD_DIM % 3 == 1
    for hd in range(A_HEADS):
        pair = slice((hd // 2) * LANE, (hd // 2 + 1) * LANE)
        out = slice(hd * LANE, (hd + 1) * LANE)
        place = (lambda z: z) if hd % 2 == 0 else (lambda z: pltpu.roll(z, HEAD_DIM, 1))
        c_hi, c_mid, c_lo = _split3(jnp.broadcast_to(c2[:, hd:hd + 1], (tm, LANE)))
        terms = jnp.where(term == 1, c_hi, jnp.where(term == 2, c_mid, c_lo))
        aug_q = jnp.where(hi_slot, terms, base_q)
        aug_k = jnp.where(lo_slot, terms, base_k)
        q_scale = jnp.broadcast_to(q_rs[:, hd:hd + 1], (tm, LANE))
        k_scale = jnp.broadcast_to(k_rs[:, hd:hd + 1], (tm, LANE))
        qa_ref[:, out] = jnp.where(data, place(q[:, pair]) * q_scale, aug_q).astype(BF16)
        ka_ref[:, out] = jnp.where(data, place(k[:, pair]) * k_scale, aug_k).astype(BF16)
        va_ref[:, out] = jnp.where(data, place(v[:, pair]), ones_v).astype(BF16)


def _inproj(x2, g, w, b, gq, gk, grp, tri, bound, t):
    n = x2.shape[0]
    tm = TM_PROJ
    aug_w = A_HEADS * LANE
    const = lambda i: (0, 0)
    row = lambda width: pl.BlockSpec((tm, width), lambda i: (i, 0))
    return pl.pallas_call(
        functools.partial(_inproj_kernel, tiles_per_seq=t // tm),
        grid=(n // tm,),
        in_specs=[
            row(D_MODEL),
            pl.BlockSpec((1, D_MODEL), const),
            pl.BlockSpec((D_MODEL, IN_W), const),
            pl.BlockSpec((1, IN_W), const),
            pl.BlockSpec((1, A_DIM), const),
            pl.BlockSpec((1, A_DIM), const),
            pl.BlockSpec((A_DIM, LANE), const),
            pl.BlockSpec((tm, tm), const),
            pl.BlockSpec((1, LANE), const),
        ],
        out_specs=[row(aug_w), row(aug_w), row(aug_w), row(REST_W), row(LANE),
                   pl.BlockSpec((1, 8, LANE), lambda i: (i, 0, 0))],
        out_shape=[
            jax.ShapeDtypeStruct((n, aug_w), BF16),
            jax.ShapeDtypeStruct((n, aug_w), BF16),
            jax.ShapeDtypeStruct((n, aug_w), BF16),
            jax.ShapeDtypeStruct((n, REST_W), BF16),
            jax.ShapeDtypeStruct((n, LANE), F32),
            jax.ShapeDtypeStruct((n // tm, 8, LANE), F32),
        ],
        scratch_shapes=[pltpu.VMEM((1, LANE), F32)],
        compiler_params=pltpu.CompilerParams(dimension_semantics=("arbitrary",), vmem_limit_bytes=VMEM_LIMIT),
        name="inproj",
    )(x2, g, w, b, gq, gk, grp, tri, bound)


def _fox_kernel(first_ref, q_ref, k_ref, v_ref, o_ref, *, tq):
    i = pl.program_id(2)
    step = (pl.program_id(0) * pl.num_programs(1) + pl.program_id(1)) * pl.num_programs(2) + i
    nt = (((1,), (1,)), ((), ()))
    half = tq // 2

    def attend(q, keys, hh, visible=None):
        cols = slice(hh * LANE, (hh + 1) * LANE)
        s = lax.dot_general(q, k_ref[keys, cols], nt, preferred_element_type=F32)
        if visible is not None:
            s = jnp.where(visible, s, NEG_INF)
        return jnp.dot(jnp.exp2(s).astype(BF16), v_ref[keys, cols], preferred_element_type=F32)

    def full_step(j, accs):
        keys = pl.ds(pl.multiple_of(j * tq, tq), tq)
        return tuple(accs[hh] + attend(q_ref[:, hh * LANE:(hh + 1) * LANE], keys, hh) for hh in range(2))

    dead_halves = first_ref[step]

    def half_step():
        keys = pl.ds(pl.multiple_of(dead_halves * half, half), half)
        return tuple(attend(q_ref[:, hh * LANE:(hh + 1) * LANE], keys, hh) for hh in range(2))

    zero = jnp.zeros((tq, LANE), F32)
    accs = lax.cond(dead_halves % 2 == 1, half_step, lambda: (zero, zero))
    accs = lax.fori_loop((dead_halves + 1) // 2, i, full_step, accs)

    off = pl.multiple_of(i * tq, tq)
    band = tq // FOX_DIAG_BANDS
    lane = lax.broadcasted_iota(jnp.int32, (band, LANE), 1)
    for r in range(FOX_DIAG_BANDS):
        rows = slice(r * band, (r + 1) * band)
        n_keys = (r + 1) * band
        visible = (lax.broadcasted_iota(jnp.int32, (band, n_keys), 1)
                   <= lax.broadcasted_iota(jnp.int32, (band, n_keys), 0) + r * band)
        outs = []
        for hh in range(2):
            cols = slice(hh * LANE, (hh + 1) * LANE)
            acc = accs[hh][rows] + attend(q_ref[rows, cols], pl.ds(off, n_keys), hh, visible)
            outs.append(acc / acc[:, HEAD_DIM:HEAD_DIM + 1])
        o_ref[rows, :] = jnp.where(lane < HEAD_DIM, outs[0], pltpu.roll(outs[1], HEAD_DIM, 1)).astype(BF16)


def _fox(qa, ka, va, c_edges, bsz, t):
    n = qa.shape[0]
    tq = TQ_FOX
    nq = t // tq
    pairs = A_HEADS // 2

    assert tq == 2 * TM_PROJ
    edges = c_edges.reshape(bsz, 2 * nq, 8, LANE)
    c_first = edges[:, 0::2, 0, :A_HEADS]
    c_last = edges[:, :, 1, :A_HEADS]
    gap = c_first[:, :, None, :] - c_last[:, None, :, :]
    dead = jnp.all((gap < -FOX_SKIP).reshape(bsz, nq, 2 * nq, pairs, 2), axis=-1)
    dead = dead & (2 * jnp.arange(nq)[None, :, None, None] > jnp.arange(2 * nq)[None, None, :, None])
    first = jnp.sum(dead, axis=2).astype(jnp.int32).transpose(0, 2, 1).reshape(-1)

    return pl.pallas_call(
        functools.partial(_fox_kernel, tq=tq),
        grid_spec=pltpu.PrefetchScalarGridSpec(
            num_scalar_prefetch=1,
            grid=(bsz, pairs, nq),
            in_specs=[
                pl.BlockSpec((tq, 2 * LANE), lambda b, p, i, first: (b * nq + i, p)),
                pl.BlockSpec((t, 2 * LANE), lambda b, p, i, first: (b, p)),
                pl.BlockSpec((t, 2 * LANE), lambda b, p, i, first: (b, p)),
            ],
            out_specs=pl.BlockSpec((tq, LANE), lambda b, p, i, first: (b * nq + i, p)),
        ),
        out_shape=jax.ShapeDtypeStruct((n, A_DIM), BF16),
        compiler_params=pltpu.CompilerParams(
            dimension_semantics=("arbitrary", "arbitrary", "arbitrary"), vmem_limit_bytes=VMEM_LIMIT),
        name="fox",
    )(first, qa, ka, va)


def _fox_safe_kernel(q_ref, k_ref, v_ref, c_ref, o_ref, *, tq):
    i = pl.program_id(2)
    q = q_ref[...]
    lane = lax.broadcasted_iota(jnp.int32, (tq, LANE), 1)
    zero = jnp.zeros_like(q)
    qh = (jnp.where(lane < HEAD_DIM, q, zero), jnp.where(lane >= HEAD_DIM, q, zero))
    tri = lax.broadcasted_iota(jnp.int32, (tq, tq), 1) <= lax.broadcasted_iota(jnp.int32, (tq, tq), 0)

    def step(j, carry, masked):
        off = pl.multiple_of(j * tq, tq)
        kt = k_ref[pl.ds(off, tq), :]
        vt = v_ref[pl.ds(off, tq), :]
        new = []
        for hh in range(2):
            m, l, acc = carry[hh]
            s = lax.dot_general(qh[hh], kt, (((1,), (1,)), ((), ())), preferred_element_type=F32)
            s = s - c_ref[0, hh:hh + 1, pl.ds(off, tq)]
            if masked:
                s = jnp.where(tri, s, NEG_INF)
            m_new = jnp.maximum(m, jnp.max(s, axis=-1, keepdims=True))
            alpha = jnp.exp2(m - m_new)
            p = jnp.exp2(s - m_new)
            l = alpha * l + jnp.sum(p, axis=-1, keepdims=True)
            acc = alpha * acc + jnp.dot(p.astype(BF16), vt, preferred_element_type=F32)
            new.append((m_new, l, acc))
        return tuple(new)

    init = tuple((jnp.full((tq, 1), NEG_INF, F32), jnp.zeros((tq, 1), F32), jnp.zeros((tq, LANE), F32))
                 for _ in range(2))
    carry = lax.fori_loop(0, i, lambda j, c: step(j, c, False), init)
    (_, l0, a0), (_, l1, a1) = step(i, carry, True)
    o_ref[...] = jnp.where(lane < HEAD_DIM, a0 / l0, a1 / l1).astype(BF16)


def _fox_safe(qa, ka, va, c_edges, bsz, t):
    del c_edges
    n = qa.shape[0]
    tq = TQ_FOX // 2
    nq = t // tq
    pairs = A_HEADS // 2
    heads = lambda z: z.reshape(n, A_HEADS, LANE)
    unpack = lambda z: heads(z)[:, :, :HEAD_DIM].reshape(n, A_DIM)
    c = jnp.sum(heads(ka)[:, :, HEAD_DIM:HEAD_DIM + 3].astype(F32), axis=-1)
    c_t = c.reshape(bsz, t, A_HEADS).transpose(0, 2, 1).reshape(bsz * pairs, 2, t)
    return pl.pallas_call(
        functools.partial(_fox_safe_kernel, tq=tq),
        grid=(bsz, pairs, nq),
        in_specs=[
            pl.BlockSpec((tq, LANE), lambda b, p, i: (b * nq + i, p)),
            pl.BlockSpec((t, LANE), lambda b, p, i: (b, p)),
            pl.BlockSpec((t, LANE), lambda b, p, i: (b, p)),
            pl.BlockSpec((1, 2, t), lambda b, p, i: (b * pairs + p, 0, 0)),
        ],
        out_specs=pl.BlockSpec((tq, LANE), lambda b, p, i: (b * nq + i, p)),
        out_shape=jax.ShapeDtypeStruct((n, A_DIM), BF16),
        compiler_params=pltpu.CompilerParams(
            dimension_semantics=("arbitrary", "arbitrary", "arbitrary"), vmem_limit_bytes=VMEM_LIMIT),
        name="fox_safe",
    )(unpack(qa), unpack(ka), unpack(va), c_t)


def _mlstm_kernel(qk_ref, halo_ref, v_ref, gates_ref, cw_ref, tri_ref, triu_ref, grp_ref, spread_ref, o_ref,
                  buf_ref, c_st, n_st, m_st, *, tc, chunk):
    j = pl.program_id(1)
    width = B_DIM

    @pl.when(j == 0)
    def _():
        c_st[...] = jnp.zeros_like(c_st)
        n_st[...] = jnp.zeros_like(n_st)
        m_st[...] = jnp.full(m_st.shape, NEG_INF, F32)

    hr = BF16_ROWS
    buf_ref[0:hr, :] = jnp.where(j == 0, 0.0, halo_ref[...].astype(F32))
    buf_ref[hr:hr + tc, :] = qk_ref[...].astype(F32)
    cw = cw_ref[...]
    y = cw[MLSTM_CONV - 1:MLSTM_CONV] * buf_ref[hr:hr + tc, :]
    for d in range(1, MLSTM_CONV):
        y = y + cw[MLSTM_CONV - 1 - d:MLSTM_CONV - d] * buf_ref[hr - d:hr - d + tc, :]
    y = y * _sigmoid(y)
    q_all = y[:, :width]
    k_all = y[:, width:] * (HEAD_DIM ** -0.5)

    head = _head_of_lane((chunk, width))
    head_sq = _head_of_lane((width, width))
    blockdiag = head_sq == lax.broadcasted_iota(jnp.int32, (width, width), 0) // HEAD_DIM
    causal = lax.broadcasted_iota(jnp.int32, (chunk, chunk), 1) <= lax.broadcasted_iota(jnp.int32, (chunk, chunk), 0)
    tri = tri_ref[...]
    grp = grp_ref[...]

    gx = _sum_select(gates_ref[...], spread_ref[...])
    i_all = gx[:, :width]
    f_all = gx[:, width:]

    local = []
    for c in range(tc // chunk):
        r0 = c * chunk
        q = q_all[r0:r0 + chunk]
        k = k_all[r0:r0 + chunk]
        vb = v_ref[r0:r0 + chunk, :]
        qb, kb = q.astype(BF16), k.astype(BF16)
        bc_x = _select_sum(tri, _log_sigmoid(f_all[r0:r0 + chunk]))
        g_t = gates_ref[r0:r0 + chunk, :].T[A_HEADS:A_HEADS + 2 * B_HEADS, :]
        bc_t = _sum_select(_log_sigmoid(g_t), triu_ref[...])

        m_cols, den_cols = [], []
        num_x = jnp.zeros((chunk, width), F32)
        for h in range(B_HEADS):
            lo = h * HEAD_DIM
            row = g_t[h:h + 1, :] - bc_t[B_HEADS + h:B_HEADS + h + 1, :]
            log_d = jnp.where(causal, bc_x[:, lo:lo + 1] + row, NEG_INF)
            m_loc = jnp.max(log_d, axis=-1, keepdims=True)
            s = lax.dot_general(jnp.where(head == h, qb, jnp.zeros_like(qb)), kb, (((1,), (1,)), ((), ())),
                                preferred_element_type=F32)
            sqk = s * jnp.exp(log_d - m_loc)
            num_x = jnp.where(head == h, jnp.dot(sqk.astype(BF16), vb, preferred_element_type=F32), num_x)
            m_cols.append(m_loc)
            den_cols.append(jnp.sum(sqk, axis=-1, keepdims=True))

        g_x = bc_x[chunk - 1:chunk, :]
        a_x = g_x - bc_x + i_all[r0:r0 + chunk]
        a_max = jnp.max(a_x, axis=0, keepdims=True)
        kw = k * jnp.exp(a_x - a_max)
        upd = lax.dot_general(kw.astype(BF16), vb, (((0,), (0,)), ((), ())), preferred_element_type=F32)
        local.append((q, qb, bc_x, _expand_heads(m_cols, head), num_x, _expand_heads(den_cols, head), g_x, a_max,
                      jnp.where(blockdiag, upd, 0.0), jnp.sum(kw, axis=0, keepdims=True)))

    for c, (q, qb, bc_x, m_loc_x, num_x, den_x, g_x, a_max, upd, k_sum) in enumerate(local):
        r0 = c * chunk
        c_prev, n_prev, m_prev = c_st[...], n_st[...], m_st[...]
        m_inter = bc_x + m_prev
        m_out_x = jnp.maximum(m_inter, m_loc_x)
        intra_w = jnp.exp(m_loc_x - m_out_x)
        inter_w = jnp.exp(m_inter - m_out_x)
        q_c = jnp.dot(qb, c_prev.astype(BF16), preferred_element_type=F32)
        q_n = _sum_select(q * n_prev, grp, parts=2)
        num = intra_w * num_x + inter_w * q_c
        den = intra_w * den_x + inter_w * q_n
        o_ref[r0:r0 + chunk, :] = (num / jnp.maximum(jnp.abs(den), jnp.exp(-m_out_x))).astype(BF16)

        m_new = jnp.maximum(g_x + m_prev, a_max)
        decay = jnp.exp(g_x + m_prev - m_new)
        fresh = jnp.exp(a_max - m_new)
        c_st[...] = decay * c_prev + fresh * upd
        n_st[...] = decay * n_prev + fresh * k_sum
        m_st[...] = m_new


def _mlstm(rest, gates, conv_w, tri_l, tri_u, grp, spread, bsz, t):
    n = rest.shape[0]
    tc, nt = TC_MLSTM, t // TC_MLSTM
    hr = BF16_ROWS
    const = lambda b, j: (0, 0)
    return pl.pallas_call(
        functools.partial(_mlstm_kernel, tc=tc, chunk=L_MLSTM),
        grid=(bsz, nt),
        in_specs=[
            pl.BlockSpec((tc, 2 * B_DIM), lambda b, j: (b * nt + j, 0)),
            pl.BlockSpec((hr, 2 * B_DIM), lambda b, j: (jnp.maximum((b * nt + j) * (tc // hr) - 1, 0), 0)),
            pl.BlockSpec((tc, B_DIM), lambda b, j: (b * nt + j, 2)),
            pl.BlockSpec((tc, LANE), lambda b, j: (b * nt + j, 0)),
            pl.BlockSpec((MLSTM_CONV, 2 * B_DIM), const),
            pl.BlockSpec((L_MLSTM, L_MLSTM), const),
            pl.BlockSpec((L_MLSTM, L_MLSTM), const),
            pl.BlockSpec((B_DIM, B_DIM), const),
            pl.BlockSpec((LANE, 2 * B_DIM), const),
        ],
        out_specs=pl.BlockSpec((tc, B_DIM), lambda b, j: (b * nt + j, 0)),
        out_shape=jax.ShapeDtypeStruct((n, B_DIM), BF16),
        scratch_shapes=[
            pltpu.VMEM((tc + hr, 2 * B_DIM), F32),
            pltpu.VMEM((B_DIM, B_DIM), F32),
            pltpu.VMEM((1, B_DIM), F32),
            pltpu.VMEM((1, B_DIM), F32),
        ],
        compiler_params=pltpu.CompilerParams(dimension_semantics=("arbitrary", "arbitrary"),
                                             vmem_limit_bytes=VMEM_LIMIT),
        name="mlstm",
    )(rest, rest, rest, gates, conv_w, tri_l, tri_u, grp, spread)


def _hgrn_kernel(q_ref, f_ref, i_ref, la_ref, l1m_ref, tri_ref, tot_ref, grp_ref, o_ref, low_ref, st_ref,
                 *, tc, chunk, direct):
    nb = q_ref.shape[0]
    width = C_DIM

    @pl.when(pl.program_id(0) == 0)
    def _():
        st_ref[...] = jnp.zeros_like(st_ref)

    head = _head_of_lane((tc, width))
    head_sq = _head_of_lane((width, width))
    blockdiag = head_sq == lax.broadcasted_iota(jnp.int32, (width, width), 0) // HEAD_DIM
    rr = lax.broadcasted_iota(jnp.int32, (tc, tc), 0)
    cc = lax.broadcasted_iota(jnp.int32, (tc, tc), 1)
    keep = (cc <= rr) & (cc // chunk == rr // chunk)

    seqs = []
    low = None
    for s in range(nb):
        xq = q_ref[s].astype(F32)
        q = xq * _sigmoid(xq)
        xf = f_ref[s].astype(F32)
        ls = _log_sigmoid(xf)
        a = la_ref[...]
        b = l1m_ref[...] + ls
        logf = jnp.maximum(a, b) + jnp.log(1.0 + jnp.exp(-jnp.abs(a - b)))
        kk = jnp.exp(l1m_ref[...] + ls - xf)
        vb = i_ref[s]

        bc = _select_sum(tri_ref[...], logf)
        blast = _select_sum(tot_ref[...], logf)
        qt = (q * jnp.exp(bc)).astype(BF16)
        kd = (kk * jnp.exp(blast - bc)).astype(BF16)

        def factored_intra(q=q, kk=kk, bc=bc, qt=qt, vb=vb):
            kt = (kk * jnp.exp(-bc)).astype(BF16)
            intra = jnp.zeros((tc, width), F32)
            for h in range(C_HEADS):
                att = lax.dot_general(jnp.where(head == h, qt, jnp.zeros_like(qt)), kt, (((1,), (1,)), ((), ())),
                                      preferred_element_type=F32)
                att = jnp.where(keep, att, 0.0).astype(BF16)
                intra = jnp.where(head == h, jnp.dot(att, vb, preferred_element_type=F32), intra)
            return intra

        def direct_intra(q=q, kk=kk, bc=bc, vb=vb):
            v = vb.astype(F32)
            pos = lax.broadcasted_iota(jnp.int32, (tc, width), 0) % chunk
            intra = jnp.zeros((tc, width), F32)
            for off in range(chunk):
                back = (lambda z: z) if off == 0 else (lambda z: pltpu.roll(z, off, 0))
                w = jnp.where(pos >= off, q * back(kk) * jnp.exp(jnp.minimum(bc - back(bc), 0.0)), 0.0)
                intra = intra + _sum_select(w, grp_ref[...]) * back(v)
            return intra

        intra = direct_intra() if direct else factored_intra()
        seqs.append((qt, kd, vb, jnp.exp(blast), intra))
        bc_min = jnp.min(bc, axis=(0, 1), keepdims=True)
        low = bc_min if low is None else jnp.minimum(low, bc_min)
    low_ref[0] = jnp.broadcast_to(low, low_ref.shape[1:])

    for c in range(tc // chunk):
        r0 = c * chunk
        for s, (qt, kd, vb, chunk_decay, intra) in enumerate(seqs):
            s_t = st_ref[s]
            inter = lax.dot_general(qt[r0:r0 + chunk], s_t.astype(BF16), (((1,), (1,)), ((), ())),
                                    preferred_element_type=F32)
            o_ref[s, r0:r0 + chunk, :] = (intra[r0:r0 + chunk] + inter).astype(BF16)
            upd = lax.dot_general(vb[r0:r0 + chunk], kd[r0:r0 + chunk], (((0,), (0,)), ((), ())),
                                  preferred_element_type=F32)
            st_ref[s] = s_t * chunk_decay[r0:r0 + 1, :] + jnp.where(blockdiag, upd, 0.0)


def _hgrn(rest, log_lb, log_1m_lb, tri_bd, tot_bd, grp, bsz, t):
    fast, low = _hgrn_call(rest, log_lb, log_1m_lb, tri_bd, tot_bd, grp, bsz, t, direct=False)
    redo = lambda: _hgrn_call(rest, log_lb, log_1m_lb, tri_bd, tot_bd, grp, bsz, t, direct=True)[0]
    return lax.cond(jnp.min(low) < -HGRN_SAFE, redo, lambda: fast)


def _hgrn_call(rest, log_lb, log_1m_lb, tri_bd, tot_bd, grp, bsz, t, direct):
    tc, nt = TC_HGRN, t // TC_HGRN
    rest3 = rest.reshape(bsz, t, REST_W)
    blk = lambda col: pl.BlockSpec((bsz, tc, C_DIM), lambda j: (0, j, col))
    const = lambda j: (0, 0)
    out, low = pl.pallas_call(
        functools.partial(_hgrn_kernel, tc=tc, chunk=L_HGRN, direct=direct),
        grid=(nt,),
        in_specs=[
            blk(4), blk(5), blk(6),
            pl.BlockSpec((1, C_DIM), const),
            pl.BlockSpec((1, C_DIM), const),
            pl.BlockSpec((tc, tc), const),
            pl.BlockSpec((tc, tc), const),
            pl.BlockSpec((C_DIM, C_DIM), const),
        ],
        out_specs=[pl.BlockSpec((bsz, tc, C_DIM), lambda j: (0, j, 0)),
                   pl.BlockSpec((1, 8, LANE), lambda j: (j, 0, 0))],
        out_shape=[jax.ShapeDtypeStruct((bsz, t, C_DIM), BF16), jax.ShapeDtypeStruct((nt, 8, LANE), F32)],
        scratch_shapes=[pltpu.VMEM((bsz, C_DIM, C_DIM), F32)],
        compiler_params=pltpu.CompilerParams(dimension_semantics=("arbitrary",), vmem_limit_bytes=VMEM_LIMIT),
        name="hgrn2_direct" if direct else "hgrn2",
    )(rest3, rest3, rest3, log_lb, log_1m_lb, tri_bd, tot_bd, grp)
    return out.reshape(bsz * t, C_DIM), low


def _outproj_kernel(x_ref, oa_ref, ob_ref, oc_ref, bo_ref, cg_ref, g_ref, grp_ref, w_ref, o_ref):
    def normed(o, lo, hi):
        return o * lax.rsqrt(_group_mean_sq(o, grp_ref) + EPS) * g_ref[:, lo:hi]

    ya = normed(oa_ref[...].astype(F32), 0, A_DIM)
    yb = _sigmoid(bo_ref[...].astype(F32)) * normed(ob_ref[...].astype(F32), A_DIM, A_DIM + B_DIM)
    cg = cg_ref[...].astype(F32)
    yc = cg * _sigmoid(cg) * normed(oc_ref[...].astype(F32), A_DIM + B_DIM, D_MIX)
    acc = x_ref[...] + jnp.dot(ya.astype(BF16), w_ref[0:A_DIM, :], preferred_element_type=F32)
    acc = acc + jnp.dot(yb.astype(BF16), w_ref[A_DIM:A_DIM + B_DIM, :], preferred_element_type=F32)
    acc = acc + jnp.dot(yc.astype(BF16), w_ref[A_DIM + B_DIM:D_MIX, :], preferred_element_type=F32)
    o_ref[...] = acc


def _outproj(x2, oa, ob, oc, rest, out_g, grp, w_out):
    n = x2.shape[0]
    tm = TM_OUT
    const = lambda i: (0, 0)
    return pl.pallas_call(
        _outproj_kernel,
        grid=(n // tm,),
        in_specs=[
            pl.BlockSpec((tm, D_MODEL), lambda i: (i, 0)),
            pl.BlockSpec((tm, A_DIM), lambda i: (i, 0)),
            pl.BlockSpec((tm, B_DIM), lambda i: (i, 0)),
            pl.BlockSpec((tm, C_DIM), lambda i: (i, 0)),
            pl.BlockSpec((tm, B_DIM), lambda i: (i, 3)),
            pl.BlockSpec((tm, C_DIM), lambda i: (i, 7)),
            pl.BlockSpec((1, D_MIX), const),
            pl.BlockSpec((A_DIM, A_DIM), const),
            pl.BlockSpec((D_MIX, D_MODEL), const),
        ],
        out_specs=pl.BlockSpec((tm, D_MODEL), lambda i: (i, 0)),
        out_shape=jax.ShapeDtypeStruct((n, D_MODEL), F32),
        compiler_params=pltpu.CompilerParams(dimension_semantics=("arbitrary",), vmem_limit_bytes=VMEM_LIMIT),
        name="outproj",
    )(x2, oa, ob, oc, rest, rest, out_g, grp, w_out)


def _ffn_kernel(x_ref, halo_ref, g_ref, wup_ref, cw_ref, cb_ref, wdn_ref, o_ref, h_ref, ug_ref, uu_ref, act_ref,
                *, tm, tiles_per_seq):
    i = pl.program_id(0)
    halo_rows = BF16_ROWS

    def rms(x):
        ms = jnp.mean(x * x, axis=-1, keepdims=True)
        return x * lax.rsqrt(ms + EPS) * g_ref[...]

    x = x_ref[...]
    first = (i % tiles_per_seq) == 0
    h_ref[0:halo_rows, :] = jnp.where(first, 0.0, rms(halo_ref[...])).astype(BF16)
    h_ref[halo_rows:, :] = rms(x).astype(BF16)
    h = h_ref[...]

    def conv(u_ref, slot, lo):
        y = cb_ref[:, lo:lo + FF_CHUNK] + cw_ref[FFN_CONV - 1:FFN_CONV, lo:lo + FF_CHUNK] * u_ref[slot, halo_rows:, :]
        for d in range(1, FFN_CONV):
            y = y + cw_ref[FFN_CONV - 1 - d:FFN_CONV - d, lo:lo + FF_CHUNK] * u_ref[slot, pl.ds(halo_rows - d, tm), :]
        return y

    def up_proj(c):
        lo = c * FF_CHUNK
        ug_ref[c % FF_SLOTS] = jnp.dot(h, wup_ref[:, lo:lo + FF_CHUNK], preferred_element_type=F32)
        uu_ref[c % FF_SLOTS] = jnp.dot(h, wup_ref[:, D_FF + lo:D_FF + lo + FF_CHUNK], preferred_element_type=F32)

    n_chunks = D_FF // FF_CHUNK
    ahead = FF_SLOTS - 1
    for c in range(min(ahead, n_chunks)):
        up_proj(c)
    for c in range(n_chunks):
        if c + ahead < n_chunks:
            up_proj(c + ahead)
        lo = c * FF_CHUNK
        gate = conv(ug_ref, c % FF_SLOTS, lo)
        up = conv(uu_ref, c % FF_SLOTS, D_FF + lo)
        act_ref[:, lo:lo + FF_CHUNK] = (gate * _sigmoid(gate) * up).astype(BF16)
    k1 = (n_chunks + 1) // 2 * FF_CHUNK
    y = x + jnp.dot(act_ref[:, 0:k1], wdn_ref[0:k1, :], preferred_element_type=F32)
    o_ref[...] = y + jnp.dot(act_ref[:, k1:D_FF], wdn_ref[k1:D_FF, :], preferred_element_type=F32)


def _ffn(x2, g, w_up, conv_w, conv_b, w_down, t):
    n = x2.shape[0]
    tm = TM_PROJ
    halo = BF16_ROWS
    const = lambda i: (0, 0)
    resident = pl.Buffered(1)
    return pl.pallas_call(
        functools.partial(_ffn_kernel, tm=tm, tiles_per_seq=t // tm),
        grid=(n // tm,),
        in_specs=[
            pl.BlockSpec((tm, D_MODEL), lambda i: (i, 0)),
            pl.BlockSpec((halo, D_MODEL), lambda i: (jnp.maximum(i * (tm // halo) - 1, 0), 0)),
            pl.BlockSpec((1, D_MODEL), const),
            pl.BlockSpec((D_MODEL, 2 * D_FF), const, pipeline_mode=resident),
            pl.BlockSpec((FFN_CONV, 2 * D_FF), const),
            pl.BlockSpec((1, 2 * D_FF), const),
            pl.BlockSpec((D_FF, D_MODEL), const, pipeline_mode=resident),
        ],
        out_specs=pl.BlockSpec((tm, D_MODEL), lambda i: (i, 0)),
        out_shape=jax.ShapeDtypeStruct((n, D_MODEL), F32),
        scratch_shapes=[
            pltpu.VMEM((tm + halo, D_MODEL), BF16),
            pltpu.VMEM((FF_SLOTS, tm + halo, FF_CHUNK), F32),
            pltpu.VMEM((FF_SLOTS, tm + halo, FF_CHUNK), F32),
            pltpu.VMEM((tm, D_FF), BF16),
        ],
        compiler_params=pltpu.CompilerParams(dimension_semantics=("arbitrary",), vmem_limit_bytes=VMEM_LIMIT),
        name="ffn",
    )(x2, x2, g, w_up, conv_w, conv_b, w_down)


def _block_diag_ones(n, blk, dtype):
    r = np.arange(n) // blk
    return jnp.asarray(r[:, None] == r[None, :], dtype)


def kernel(x, lb_logits, norm_mix_g, w_in, b_in, a_q_g, a_k_g, b_conv_w, out_g, w_out, norm_ffn_g, w_up,
           ffn_conv_w, ffn_conv_b, w_down):
    bsz, t, d = x.shape
    depth = w_in.shape[0]
    assert d == D_MODEL and t % TQ_FOX == 0 and t % TC_MLSTM == 0 and t % TC_HGRN == 0
    n = bsz * t
    x2 = x.reshape(n, d).astype(F32)

    p = jax.nn.softmax(lb_logits.astype(F32), axis=0)
    lb_all = jnp.maximum(jnp.cumsum(p, axis=0) - p[0], 0.0)
    log_lb = jnp.log(lb_all)
    log_1m_lb = jnp.log1p(-lb_all)

    grp_a = _block_diag_ones(A_DIM, HEAD_DIM, BF16)
    head_sel = jnp.asarray(np.arange(A_DIM)[:, None] // HEAD_DIM == np.arange(LANE)[None, :], BF16)
    grp_b = _block_diag_ones(B_DIM, HEAD_DIM, BF16)
    ar = np.arange(L_MLSTM)
    tri_u = jnp.asarray(ar[:, None] <= ar[None, :], BF16)
    tri_l = jnp.asarray(ar[:, None] >= ar[None, :], BF16)
    gate_row = np.arange(LANE)[:, None] - A_HEADS
    spread = jnp.asarray(gate_row == np.arange(2 * B_DIM)[None, :] // HEAD_DIM, BF16)
    ap = np.arange(TM_PROJ)
    tri_proj = jnp.asarray(ap[:, None] >= ap[None, :], BF16)
    ah = np.arange(TC_HGRN)
    same = (ah[:, None] // L_HGRN) == (ah[None, :] // L_HGRN)
    tri_bd = jnp.asarray(same & (ah[:, None] >= ah[None, :]), BF16)
    tot_bd = jnp.asarray(same, BF16)

    w_in_r = _reorder_in_cols(w_in).astype(BF16)
    for l in range(depth):
        w_r = w_in_r[l]
        b_r = _reorder_in_cols(b_in[l].astype(F32))[None, :]
        gq = (jnp.tile(a_q_g[l].astype(F32), A_HEADS) * (HEAD_DIM ** -0.5 * LOG2E))[None, :]
        gk = jnp.tile(a_k_g[l].astype(F32), A_HEADS)[None, :]
        bound = HEAD_DIM * jnp.max(jnp.abs(gq)) * jnp.max(jnp.abs(gk))
        qa, ka, va, rest, gates, c_edges = _inproj(x2, norm_mix_g[l][None, :].astype(F32), w_r, b_r, gq, gk, head_sel, tri_proj,
                                          jnp.full((1, LANE), bound, F32), t)
        oa = lax.cond(bound < FOX_BOUND_MAX, functools.partial(_fox, bsz=bsz, t=t),
                      functools.partial(_fox_safe, bsz=bsz, t=t), qa, ka, va, c_edges)
        ob = _mlstm(rest, gates, b_conv_w[l].astype(F32), tri_l, tri_u, grp_b, spread, bsz, t)
        oc = _hgrn(rest, log_lb[l][None, :], log_1m_lb[l][None, :], tri_bd, tot_bd, grp_b, bsz, t)
        x2 = _outproj(x2, oa, ob, oc, rest, out_g[l][None, :].astype(F32), grp_a, w_out[l].astype(BF16))
        x2 = _ffn(x2, norm_ffn_g[l][None, :].astype(F32), w_up[l].astype(BF16), ffn_conv_w[l].astype(F32),
                  ffn_conv_b[l][None, :].astype(F32), w_down[l].astype(BF16), t)
    return x2.reshape(bsz, t, d).astype(x.dtype)
```

```python
import functools

import numpy as np
import jax
import jax.numpy as jnp
from jax import lax
from jax.experimental import pallas as pl
from jax.experimental.pallas import tpu as pltpu

F32 = jnp.float32
BF16 = jnp.bfloat16
NEG_INF = float("-inf")
LOG2E = 1.4426950408889634

D_MODEL = 1024
HEAD_DIM = 64
A_HEADS, B_HEADS, C_HEADS = 8, 4, 4
A_DIM, B_DIM, C_DIM = A_HEADS * HEAD_DIM, B_HEADS * HEAD_DIM, C_HEADS * HEAD_DIM
D_MIX = A_DIM + B_DIM + C_DIM
D_FF = 2816
MLSTM_CONV = 4
FFN_CONV = 3
EPS = 1e-6

LANE = 128
BF16_ROWS = 16
VMEM_LIMIT = 56 * 1024 * 1024

QKV_W = 3 * A_DIM
REST_W = 8 * B_DIM
IN_W = QKV_W + REST_W + LANE

TM_PROJ = 512
TM_OUT = 1024
TQ_FOX = 1024
FOX_DIAG_BANDS = 2
FOX_BOUND_MAX = 40.0
FOX_SKIP = 160.0
TC_MLSTM = 1024
L_MLSTM = 128
TC_HGRN = 256
L_HGRN = 16
HGRN_SAFE = 60.0
FF_CHUNK = 256
FF_SLOTS = 2


def _reorder_in_cols(w):
    a_f = 3 * A_DIM
    b_qk = a_f + A_HEADS
    b_v = b_qk + 2 * B_DIM
    b_i = b_v + B_DIM
    b_f = b_i + B_HEADS
    b_o = b_f + B_HEADS
    c_q = b_o + B_DIM
    end = c_q + 4 * C_DIM
    sl = lambda lo, hi: lax.slice_in_dim(w, lo, hi, axis=-1)
    n_gate = A_HEADS + 2 * B_HEADS
    pad = jnp.zeros(w.shape[:-1] + (LANE - n_gate,), w.dtype)
    out = jnp.concatenate([sl(0, a_f), sl(b_qk, b_i), sl(b_o, end), sl(a_f, b_qk), sl(b_i, b_o), pad], axis=-1)
    assert out.shape[-1] == IN_W
    return out


def _log_sigmoid(x):
    return jnp.minimum(x, 0.0) - jnp.log(1.0 + jnp.exp(-jnp.abs(x)))


def _sigmoid(x):
    return 1.0 / (1.0 + jnp.exp(-x))


def _head_of_lane(shape):
    return lax.broadcasted_iota(jnp.int32, shape, len(shape) - 1) // HEAD_DIM


def _expand_heads(cols, head):
    out = cols[-1]
    for h in range(len(cols) - 2, -1, -1):
        out = jnp.where(head == h, cols[h], out)
    return out


def _group_mean_sq(z, grp_ref):
    w = z.shape[-1]
    zz = (z * z).astype(BF16)
    return jnp.dot(zz, grp_ref[0:w, 0:w], preferred_element_type=F32) * (1.0 / HEAD_DIM)


def _split3(x):
    hi = x.astype(BF16).astype(F32)
    mid = (x - hi).astype(BF16).astype(F32)
    lo = (x - hi - mid).astype(BF16).astype(F32)
    return hi, mid, lo


def _select_sum(sel, x, parts=3):
    out = None
    for term in _split3(x)[:parts]:
        d = jnp.dot(sel, term.astype(BF16), preferred_element_type=F32)
        out = d if out is None else out + d
    return out


def _sum_select(x, sel, parts=3):
    out = None
    for term in _split3(x)[:parts]:
        d = jnp.dot(term.astype(BF16), sel, preferred_element_type=F32)
        out = d if out is None else out + d
    return out


def _inproj_kernel(x_ref, g_ref, w_ref, b_ref, gq_ref, gk_ref, grp_ref, tri_ref, bound_ref,
                   qa_ref, ka_ref, va_ref, rest_ref, gates_ref, edges_ref, carry_ref, *, tiles_per_seq):
    i = pl.program_id(0)
    tm = x_ref.shape[0]

    @pl.when(i % tiles_per_seq == 0)
    def _():
        carry_ref[...] = jnp.zeros_like(carry_ref)

    x = x_ref[...]
    ms = jnp.mean(x * x, axis=-1, keepdims=True)
    h = (x * lax.rsqrt(ms + EPS) * g_ref[...]).astype(BF16)

    def proj(lo, hi):
        return jnp.dot(h, w_ref[:, lo:hi], preferred_element_type=F32) + b_ref[:, lo:hi]

    gates = proj(QKV_W + REST_W, IN_W)
    gates_ref[...] = gates
    q = proj(0, A_DIM)
    k = proj(A_DIM, 2 * A_DIM)
    v = proj(2 * A_DIM, QKV_W)

    cs = _select_sum(tri_ref[...], _log_sigmoid(gates))
    c2 = (cs + carry_ref[...]) * LOG2E
    carry_ref[...] = carry_ref[...] + cs[tm - 1:tm, :]
    edge_row = lax.broadcasted_iota(jnp.int32, (8, LANE), 0)
    edges_ref[0] = jnp.where(edge_row == 0, c2[0:1, :], jnp.where(edge_row == 1, c2[tm - 1:tm, :], 0.0))

    def head_rsqrt(z):
        ss = jnp.dot((z * z).astype(BF16), grp_ref[...], preferred_element_type=F32) * (1.0 / HEAD_DIM)
        return lax.rsqrt(ss + EPS)

    q_rs, k_rs = head_rsqrt(q), head_rsqrt(k)
    q = q * gq_ref[...]
    k = k * gk_ref[...]
    for c in range(REST_W // A_DIM):
        rest_ref[:, c * A_DIM:(c + 1) * A_DIM] = proj(QKV_W + c * A_DIM, QKV_W + (c + 1) * A_DIM).astype(BF16)

    lane = lax.broadcasted_iota(jnp.int32, (tm, LANE), 1)
    lo_slot = (lane >= HEAD_DIM) & (lane < HEAD_DIM + 3)
    hi_slot = (lane >= HEAD_DIM + 3) & (lane < HEAD_DIM + 6)
    last = lane == HEAD_DIM + 6
    base_q = jnp.where(lo_slot, -1.0, jnp.where(last, 1.0, 0.0))
    base_k = jnp.where(hi_slot, 1.0, jnp.where(last, -bound_ref[...], 0.0))
    ones_v = jnp.where(lane == HEAD_DIM, 1.0, 0.0)
    data = lane < HEAD_DIM
    term = lane % 3
    assert HEAD_DIM % 3 == 1
    for hd in range(A_HEADS):
        pair = slice((hd // 2) * LANE, (hd // 2 + 1) * LANE)
        out = slice(hd * LANE, (hd + 1) * LANE)
        place = (lambda z: z) if hd % 2 == 0 else (lambda z: pltpu.roll(z, HEAD_DIM, 1))
        c_hi, c_mid, c_lo = _split3(jnp.broadcast_to(c2[:, hd:hd + 1], (tm, LANE)))
        terms = jnp.where(term == 1, c_hi, jnp.where(term == 2, c_mid, c_lo))
        aug_q = jnp.where(hi_slot, terms, base_q)
        aug_k = jnp.where(lo_slot, terms, base_k)
        q_scale = jnp.broadcast_to(q_rs[:, hd:hd + 1], (tm, LANE))
        k_scale = jnp.broadcast_to(k_rs[:, hd:hd + 1], (tm, LANE))
        qa_ref[:, out] = jnp.where(data, place(q[:, pair]) * q_scale, aug_q).astype(BF16)
        ka_ref[:, out] = jnp.where(data, place(k[:, pair]) * k_scale, aug_k).astype(BF16)
        va_ref[:, out] = jnp.where(data, place(v[:, pair]), ones_v).astype(BF16)


def _inproj(x2, g, w, b, gq, gk, grp, tri, bound, t):
    n = x2.shape[0]
    tm = TM_PROJ
    aug_w = A_HEADS * LANE
    const = lambda i: (0, 0)
    row = lambda width: pl.BlockSpec((tm, width), lambda i: (i, 0))
    return pl.pallas_call(
        functools.partial(_inproj_kernel, tiles_per_seq=t // tm),
        grid=(n // tm,),
        in_specs=[
            row(D_MODEL),
            pl.BlockSpec((1, D_MODEL), const),
            pl.BlockSpec((D_MODEL, IN_W), const),
            pl.BlockSpec((1, IN_W), const),
            pl.BlockSpec((1, A_DIM), const),
            pl.BlockSpec((1, A_DIM), const),
            pl.BlockSpec((A_DIM, LANE), const),
            pl.BlockSpec((tm, tm), const),
            pl.BlockSpec((1, LANE), const),
        ],
        out_specs=[row(aug_w), row(aug_w), row(aug_w), row(REST_W), row(LANE),
                   pl.BlockSpec((1, 8, LANE), lambda i: (i, 0, 0))],
        out_shape=[
            jax.ShapeDtypeStruct((n, aug_w), BF16),
            jax.ShapeDtypeStruct((n, aug_w), BF16),
            jax.ShapeDtypeStruct((n, aug_w), BF16),
            jax.ShapeDtypeStruct((n, REST_W), BF16),
            jax.ShapeDtypeStruct((n, LANE), F32),
            jax.ShapeDtypeStruct((n // tm, 8, LANE), F32),
        ],
        scratch_shapes=[pltpu.VMEM((1, LANE), F32)],
        compiler_params=pltpu.CompilerParams(dimension_semantics=("arbitrary",), vmem_limit_bytes=VMEM_LIMIT),
        name="inproj",
    )(x2, g, w, b, gq, gk, grp, tri, bound)


def _fox_kernel(first_ref, q_ref, k_ref, v_ref, o_ref, *, tq):
    i = pl.program_id(2)
    step = (pl.program_id(0) * pl.num_programs(1) + pl.program_id(1)) * pl.num_programs(2) + i
    nt = (((1,), (1,)), ((), ()))
    half = tq // 2

    def attend(q, keys, hh, visible=None):
        cols = slice(hh * LANE, (hh + 1) * LANE)
        s = lax.dot_general(q, k_ref[keys, cols], nt, preferred_element_type=F32)
        if visible is not None:
            s = jnp.where(visible, s, NEG_INF)
        return jnp.dot(jnp.exp2(s).astype(BF16), v_ref[keys, cols], preferred_element_type=F32)

    def full_step(j, accs):
        keys = pl.ds(pl.multiple_of(j * tq, tq), tq)
        return tuple(accs[hh] + attend(q_ref[:, hh * LANE:(hh + 1) * LANE], keys, hh) for hh in range(2))

    dead_halves = first_ref[step]

    def half_step():
        keys = pl.ds(pl.multiple_of(dead_halves * half, half), half)
        return tuple(attend(q_ref[:, hh * LANE:(hh + 1) * LANE], keys, hh) for hh in range(2))

    zero = jnp.zeros((tq, LANE), F32)
    accs = lax.cond(dead_halves % 2 == 1, half_step, lambda: (zero, zero))
    accs = lax.fori_loop((dead_halves + 1) // 2, i, full_step, accs)

    off = pl.multiple_of(i * tq, tq)
    band = tq // FOX_DIAG_BANDS
    lane = lax.broadcasted_iota(jnp.int32, (band, LANE), 1)
    for r in range(FOX_DIAG_BANDS):
        rows = slice(r * band, (r + 1) * band)
        n_keys = (r + 1) * band
        visible = (lax.broadcasted_iota(jnp.int32, (band, n_keys), 1)
                   <= lax.broadcasted_iota(jnp.int32, (band, n_keys), 0) + r * band)
        outs = []
        for hh in range(2):
            cols = slice(hh * LANE, (hh + 1) * LANE)
            acc = accs[hh][rows] + attend(q_ref[rows, cols], pl.ds(off, n_keys), hh, visible)
            outs.append(acc / acc[:, HEAD_DIM:HEAD_DIM + 1])
        o_ref[rows, :] = jnp.where(lane < HEAD_DIM, outs[0], pltpu.roll(outs[1], HEAD_DIM, 1)).astype(BF16)


def _fox(qa, ka, va, c_edges, bsz, t):
    n = qa.shape[0]
    tq = TQ_FOX
    nq = t // tq
    pairs = A_HEADS // 2

    assert tq == 2 * TM_PROJ
    edges = c_edges.reshape(bsz, 2 * nq, 8, LANE)
    c_first = edges[:, 0::2, 0, :A_HEADS]
    c_last = edges[:, :, 1, :A_HEADS]
    gap = c_first[:, :, None, :] - c_last[:, None, :, :]
    dead = jnp.all((gap < -FOX_SKIP).reshape(bsz, nq, 2 * nq, pairs, 2), axis=-1)
    dead = dead & (2 * jnp.arange(nq)[None, :, None, None] > jnp.arange(2 * nq)[None, None, :, None])
    first = jnp.sum(dead, axis=2).astype(jnp.int32).transpose(0, 2, 1).reshape(-1)

    return pl.pallas_call(
        functools.partial(_fox_kernel, tq=tq),
        grid_spec=pltpu.PrefetchScalarGridSpec(
            num_scalar_prefetch=1,
            grid=(bsz, pairs, nq),
            in_specs=[
                pl.BlockSpec((tq, 2 * LANE), lambda b, p, i, first: (b * nq + i, p)),
                pl.BlockSpec((t, 2 * LANE), lambda b, p, i, first: (b, p)),
                pl.BlockSpec((t, 2 * LANE), lambda b, p, i, first: (b, p)),
            ],
            out_specs=pl.BlockSpec((tq, LANE), lambda b, p, i, first: (b * nq + i, p)),
        ),
        out_shape=jax.ShapeDtypeStruct((n, A_DIM), BF16),
        compiler_params=pltpu.CompilerParams(
            dimension_semantics=("arbitrary", "arbitrary", "arbitrary"), vmem_limit_bytes=VMEM_LIMIT),
        name="fox",
    )(first, qa, ka, va)


def _fox_safe_kernel(q_ref, k_ref, v_ref, c_ref, o_ref, *, tq):
    i = pl.program_id(2)
    q = q_ref[...]
    lane = lax.broadcasted_iota(jnp.int32, (tq, LANE), 1)
    zero = jnp.zeros_like(q)
    qh = (jnp.where(lane < HEAD_DIM, q, zero), jnp.where(lane >= HEAD_DIM, q, zero))
    tri = lax.broadcasted_iota(jnp.int32, (tq, tq), 1) <= lax.broadcasted_iota(jnp.int32, (tq, tq), 0)

    def step(j, carry, masked):
        off = pl.multiple_of(j * tq, tq)
        kt = k_ref[pl.ds(off, tq), :]
        vt = v_ref[pl.ds(off, tq), :]
        new = []
        for hh in range(2):
            m, l, acc = carry[hh]
            s = lax.dot_general(qh[hh], kt, (((1,), (1,)), ((), ())), preferred_element_type=F32)
            s = s - c_ref[0, hh:hh + 1, pl.ds(off, tq)]
            if masked:
                s = jnp.where(tri, s, NEG_INF)
            m_new = jnp.maximum(m, jnp.max(s, axis=-1, keepdims=True))
            alpha = jnp.exp2(m - m_new)
            p = jnp.exp2(s - m_new)
            l = alpha * l + jnp.sum(p, axis=-1, keepdims=True)
            acc = alpha * acc + jnp.dot(p.astype(BF16), vt, preferred_element_type=F32)
            new.append((m_new, l, acc))
        return tuple(new)

    init = tuple((jnp.full((tq, 1), NEG_INF, F32), jnp.zeros((tq, 1), F32), jnp.zeros((tq, LANE), F32))
                 for _ in range(2))
    carry = lax.fori_loop(0, i, lambda j, c: step(j, c, False), init)
    (_, l0, a0), (_, l1, a1) = step(i, carry, True)
    o_ref[...] = jnp.where(lane < HEAD_DIM, a0 / l0, a1 / l1).astype(BF16)


def _fox_safe(qa, ka, va, c_edges, bsz, t):
    del c_edges
    n = qa.shape[0]
    tq = TQ_FOX // 2
    nq = t // tq
    pairs = A_HEADS // 2
    heads = lambda z: z.reshape(n, A_HEADS, LANE)
    unpack = lambda z: heads(z)[:, :, :HEAD_DIM].reshape(n, A_DIM)
    c = jnp.sum(heads(ka)[:, :, HEAD_DIM:HEAD_DIM + 3].astype(F32), axis=-1)
    c_t = c.reshape(bsz, t, A_HEADS).transpose(0, 2, 1).reshape(bsz * pairs, 2, t)
    return pl.pallas_call(
        functools.partial(_fox_safe_kernel, tq=tq),
        grid=(bsz, pairs, nq),
        in_specs=[
            pl.BlockSpec((tq, LANE), lambda b, p, i: (b * nq + i, p)),
            pl.BlockSpec((t, LANE), lambda b, p, i: (b, p)),
            pl.BlockSpec((t, LANE), lambda b, p, i: (b, p)),
            pl.BlockSpec((1, 2, t), lambda b, p, i: (b * pairs + p, 0, 0)),
        ],
        out_specs=pl.BlockSpec((tq, LANE), lambda b, p, i: (b * nq + i, p)),
        out_shape=jax.ShapeDtypeStruct((n, A_DIM), BF16),
        compiler_params=pltpu.CompilerParams(
            dimension_semantics=("arbitrary", "arbitrary", "arbitrary"), vmem_limit_bytes=VMEM_LIMIT),
        name="fox_safe",
    )(unpack(qa), unpack(ka), unpack(va), c_t)


def _mlstm_kernel(qk_ref, halo_ref, v_ref, gates_ref, cw_ref, tri_ref, triu_ref, grp_ref, spread_ref, o_ref,
                  buf_ref, c_st, n_st, m_st, *, tc, chunk):
    j = pl.program_id(1)
    width = B_DIM

    @pl.when(j == 0)
    def _():
        c_st[...] = jnp.zeros_like(c_st)
        n_st[...] = jnp.zeros_like(n_st)
        m_st[...] = jnp.full(m_st.shape, NEG_INF, F32)

    hr = BF16_ROWS
    buf_ref[0:hr, :] = jnp.where(j == 0, 0.0, halo_ref[...].astype(F32))
    buf_ref[hr:hr + tc, :] = qk_ref[...].astype(F32)
    cw = cw_ref[...]
    y = cw[MLSTM_CONV - 1:MLSTM_CONV] * buf_ref[hr:hr + tc, :]
    for d in range(1, MLSTM_CONV):
        y = y + cw[MLSTM_CONV - 1 - d:MLSTM_CONV - d] * buf_ref[hr - d:hr - d + tc, :]
    y = y * _sigmoid(y)
    q_all = y[:, :width]
    k_all = y[:, width:] * (HEAD_DIM ** -0.5)

    head = _head_of_lane((chunk, width))
    head_sq = _head_of_lane((width, width))
    blockdiag = head_sq == lax.broadcasted_iota(jnp.int32, (width, width), 0) // HEAD_DIM
    causal = lax.broadcasted_iota(jnp.int32, (chunk, chunk), 1) <= lax.broadcasted_iota(jnp.int32, (chunk, chunk), 0)
    tri = tri_ref[...]
    grp = grp_ref[...]

    gx = _sum_select(gates_ref[...], spread_ref[...])
    i_all = gx[:, :width]
    f_all = gx[:, width:]

    local = []
    for c in range(tc // chunk):
        r0 = c * chunk
        q = q_all[r0:r0 + chunk]
        k = k_all[r0:r0 + chunk]
        vb = v_ref[r0:r0 + chunk, :]
        qb, kb = q.astype(BF16), k.astype(BF16)
        bc_x = _select_sum(tri, _log_sigmoid(f_all[r0:r0 + chunk]))
        g_t = gates_ref[r0:r0 + chunk, :].T[A_HEADS:A_HEADS + 2 * B_HEADS, :]
        bc_t = _sum_select(_log_sigmoid(g_t), triu_ref[...])

        m_cols, den_cols = [], []
        num_x = jnp.zeros((chunk, width), F32)
        for h in range(B_HEADS):
            lo = h * HEAD_DIM
            row = g_t[h:h + 1, :] - bc_t[B_HEADS + h:B_HEADS + h + 1, :]
            log_d = jnp.where(causal, bc_x[:, lo:lo + 1] + row, NEG_INF)
            m_loc = jnp.max(log_d, axis=-1, keepdims=True)
            s = lax.dot_general(jnp.where(head == h, qb, jnp.zeros_like(qb)), kb, (((1,), (1,)), ((), ())),
                                preferred_element_type=F32)
            sqk = s * jnp.exp(log_d - m_loc)
            num_x = jnp.where(head == h, jnp.dot(sqk.astype(BF16), vb, preferred_element_type=F32), num_x)
            m_cols.append(m_loc)
            den_cols.append(jnp.sum(sqk, axis=-1, keepdims=True))

        g_x = bc_x[chunk - 1:chunk, :]
        a_x = g_x - bc_x + i_all[r0:r0 + chunk]
        a_max = jnp.max(a_x, axis=0, keepdims=True)
        kw = k * jnp.exp(a_x - a_max)
        upd = lax.dot_general(kw.astype(BF16), vb, (((0,), (0,)), ((), ())), preferred_element_type=F32)
        local.append((q, qb, bc_x, _expand_heads(m_cols, head), num_x, _expand_heads(den_cols, head), g_x, a_max,
                      jnp.where(blockdiag, upd, 0.0), jnp.sum(kw, axis=0, keepdims=True)))

    for c, (q, qb, bc_x, m_loc_x, num_x, den_x, g_x, a_max, upd, k_sum) in enumerate(local):
        r0 = c * chunk
        c_prev, n_prev, m_prev = c_st[...], n_st[...], m_st[...]
        m_inter = bc_x + m_prev
        m_out_x = jnp.maximum(m_inter, m_loc_x)
        intra_w = jnp.exp(m_loc_x - m_out_x)
        inter_w = jnp.exp(m_inter - m_out_x)
        q_c = jnp.dot(qb, c_prev.astype(BF16), preferred_element_type=F32)
        q_n = _sum_select(q * n_prev, grp, parts=2)
        num = intra_w * num_x + inter_w * q_c
        den = intra_w * den_x + inter_w * q_n
        o_ref[r0:r0 + chunk, :] = (num / jnp.maximum(jnp.abs(den), jnp.exp(-m_out_x))).astype(BF16)

        m_new = jnp.maximum(g_x + m_prev, a_max)
        decay = jnp.exp(g_x + m_prev - m_new)
        fresh = jnp.exp(a_max - m_new)
        c_st[...] = decay * c_prev + fresh * upd
        n_st[...] = decay * n_prev + fresh * k_sum
        m_st[...] = m_new


def _mlstm(rest, gates, conv_w, tri_l, tri_u, grp, spread, bsz, t):
    n = rest.shape[0]
    tc, nt = TC_MLSTM, t // TC_MLSTM
    hr = BF16_ROWS
    const = lambda b, j: (0, 0)
    return pl.pallas_call(
        functools.partial(_mlstm_kernel, tc=tc, chunk=L_MLSTM),
        grid=(bsz, nt),
        in_specs=[
            pl.BlockSpec((tc, 2 * B_DIM), lambda b, j: (b * nt + j, 0)),
            pl.BlockSpec((hr, 2 * B_DIM), lambda b, j: (jnp.maximum((b * nt + j) * (tc // hr) - 1, 0), 0)),
            pl.BlockSpec((tc, B_DIM), lambda b, j: (b * nt + j, 2)),
            pl.BlockSpec((tc, LANE), lambda b, j: (b * nt + j, 0)),
            pl.BlockSpec((MLSTM_CONV, 2 * B_DIM), const),
            pl.BlockSpec((L_MLSTM, L_MLSTM), const),
            pl.BlockSpec((L_MLSTM, L_MLSTM), const),
            pl.BlockSpec((B_DIM, B_DIM), const),
            pl.BlockSpec((LANE, 2 * B_DIM), const),
        ],
        out_specs=pl.BlockSpec((tc, B_DIM), lambda b, j: (b * nt + j, 0)),
        out_shape=jax.ShapeDtypeStruct((n, B_DIM), BF16),
        scratch_shapes=[
            pltpu.VMEM((tc + hr, 2 * B_DIM), F32),
            pltpu.VMEM((B_DIM, B_DIM), F32),
            pltpu.VMEM((1, B_DIM), F32),
            pltpu.VMEM((1, B_DIM), F32),
        ],
        compiler_params=pltpu.CompilerParams(dimension_semantics=("arbitrary", "arbitrary"),
                                             vmem_limit_bytes=VMEM_LIMIT),
        name="mlstm",
    )(rest, rest, rest, gates, conv_w, tri_l, tri_u, grp, spread)


def _hgrn_kernel(q_ref, f_ref, i_ref, la_ref, l1m_ref, tri_ref, tot_ref, grp_ref, o_ref, low_ref, st_ref,
                 *, tc, chunk, direct):
    nb = q_ref.shape[0]
    width = C_DIM

    @pl.when(pl.program_id(0) == 0)
    def _():
        st_ref[...] = jnp.zeros_like(st_ref)

    head = _head_of_lane((tc, width))
    head_sq = _head_of_lane((width, width))
    blockdiag = head_sq == lax.broadcasted_iota(jnp.int32, (width, width), 0) // HEAD_DIM
    rr = lax.broadcasted_iota(jnp.int32, (tc, tc), 0)
    cc = lax.broadcasted_iota(jnp.int32, (tc, tc), 1)
    keep = (cc <= rr) & (cc // chunk == rr // chunk)

    seqs = []
    low = None
    for s in range(nb):
        xq = q_ref[s].astype(F32)
        q = xq * _sigmoid(xq)
        xf = f_ref[s].astype(F32)
        ls = _log_sigmoid(xf)
        a = la_ref[...]
        b = l1m_ref[...] + ls
        logf = jnp.maximum(a, b) + jnp.log(1.0 + jnp.exp(-jnp.abs(a - b)))
        kk = jnp.exp(l1m_ref[...] + ls - xf)
        vb = i_ref[s]

        bc = _select_sum(tri_ref[...], logf)
        blast = _select_sum(tot_ref[...], logf)
        qt = (q * jnp.exp(bc)).astype(BF16)
        kd = (kk * jnp.exp(blast - bc)).astype(BF16)

        def factored_intra(q=q, kk=kk, bc=bc, qt=qt, vb=vb):
            kt = (kk * jnp.exp(-bc)).astype(BF16)
            intra = jnp.zeros((tc, width), F32)
            for h in range(C_HEADS):
                att = lax.dot_general(jnp.where(head == h, qt, jnp.zeros_like(qt)), kt, (((1,), (1,)), ((), ())),
                                      preferred_element_type=F32)
                att = jnp.where(keep, att, 0.0).astype(BF16)
                intra = jnp.where(head == h, jnp.dot(att, vb, preferred_element_type=F32), intra)
            return intra

        def direct_intra(q=q, kk=kk, bc=bc, vb=vb):
            v = vb.astype(F32)
            pos = lax.broadcasted_iota(jnp.int32, (tc, width), 0) % chunk
            intra = jnp.zeros((tc, width), F32)
            for off in range(chunk):
                back = (lambda z: z) if off == 0 else (lambda z: pltpu.roll(z, off, 0))
                w = jnp.where(pos >= off, q * back(kk) * jnp.exp(jnp.minimum(bc - back(bc), 0.0)), 0.0)
                intra = intra + _sum_select(w, grp_ref[...]) * back(v)
            return intra

        intra = direct_intra() if direct else factored_intra()
        seqs.append((qt, kd, vb, jnp.exp(blast), intra))
        bc_min = jnp.min(bc, axis=(0, 1), keepdims=True)
        low = bc_min if low is None else jnp.minimum(low, bc_min)
    low_ref[0] = jnp.broadcast_to(low, low_ref.shape[1:])

    for c in range(tc // chunk):
        r0 = c * chunk
        for s, (qt, kd, vb, chunk_decay, intra) in enumerate(seqs):
            s_t = st_ref[s]
            inter = lax.dot_general(qt[r0:r0 + chunk], s_t.astype(BF16), (((1,), (1,)), ((), ())),
                                    preferred_element_type=F32)
            o_ref[s, r0:r0 + chunk, :] = (intra[r0:r0 + chunk] + inter).astype(BF16)
            upd = lax.dot_general(vb[r0:r0 + chunk], kd[r0:r0 + chunk], (((0,), (0,)), ((), ())),
                                  preferred_element_type=F32)
            st_ref[s] = s_t * chunk_decay[r0:r0 + 1, :] + jnp.where(blockdiag, upd, 0.0)


def _hgrn(rest, log_lb, log_1m_lb, tri_bd, tot_bd, grp, bsz, t):
    fast, low = _hgrn_call(rest, log_lb, log_1m_lb, tri_bd, tot_bd, grp, bsz, t, direct=False)
    redo = lambda: _hgrn_call(rest, log_lb, log_1m_lb, tri_bd, tot_bd, grp, bsz, t, direct=True)[0]
    return lax.cond(jnp.min(low) < -HGRN_SAFE, redo, lambda: fast)


def _hgrn_call(rest, log_lb, log_1m_lb, tri_bd, tot_bd, grp, bsz, t, direct):
    tc, nt = TC_HGRN, t // TC_HGRN
    rest3 = rest.reshape(bsz, t, REST_W)
    blk = lambda col: pl.BlockSpec((bsz, tc, C_DIM), lambda j: (0, j, col))
    const = lambda j: (0, 0)
    out, low = pl.pallas_call(
        functools.partial(_hgrn_kernel, tc=tc, chunk=L_HGRN, direct=direct),
        grid=(nt,),
        in_specs=[
            blk(4), blk(5), blk(6),
            pl.BlockSpec((1, C_DIM), const),
            pl.BlockSpec((1, C_DIM), const),
            pl.BlockSpec((tc, tc), const),
            pl.BlockSpec((tc, tc), const),
            pl.BlockSpec((C_DIM, C_DIM), const),
        ],
        out_specs=[pl.BlockSpec((bsz, tc, C_DIM), lambda j: (0, j, 0)),
                   pl.BlockSpec((1, 8, LANE), lambda j: (j, 0, 0))],
        out_shape=[jax.ShapeDtypeStruct((bsz, t, C_DIM), BF16), jax.ShapeDtypeStruct((nt, 8, LANE), F32)],
        scratch_shapes=[pltpu.VMEM((bsz, C_DIM, C_DIM), F32)],
        compiler_params=pltpu.CompilerParams(dimension_semantics=("arbitrary",), vmem_limit_bytes=VMEM_LIMIT),
        name="hgrn2_direct" if direct else "hgrn2",
    )(rest3, rest3, rest3, log_lb, log_1m_lb, tri_bd, tot_bd, grp)
    return out.reshape(bsz * t, C_DIM), low


def _outproj_kernel(x_ref, oa_ref, ob_ref, oc_ref, bo_ref, cg_ref, g_ref, grp_ref, w_ref, o_ref):
    def normed(o, lo, hi):
        return o * lax.rsqrt(_group_mean_sq(o, grp_ref) + EPS) * g_ref[:, lo:hi]

    ya = normed(oa_ref[...].astype(F32), 0, A_DIM)
    yb = _sigmoid(bo_ref[...].astype(F32)) * normed(ob_ref[...].astype(F32), A_DIM, A_DIM + B_DIM)
    cg = cg_ref[...].astype(F32)
    yc = cg * _sigmoid(cg) * normed(oc_ref[...].astype(F32), A_DIM + B_DIM, D_MIX)
    acc = x_ref[...] + jnp.dot(ya.astype(BF16), w_ref[0:A_DIM, :], preferred_element_type=F32)
    acc = acc + jnp.dot(yb.astype(BF16), w_ref[A_DIM:A_DIM + B_DIM, :], preferred_element_type=F32)
    acc = acc + jnp.dot(yc.astype(BF16), w_ref[A_DIM + B_DIM:D_MIX, :], preferred_element_type=F32)
    o_ref[...] = acc


def _outproj(x2, oa, ob, oc, rest, out_g, grp, w_out):
    n = x2.shape[0]
    tm = TM_OUT
    const = lambda i: (0, 0)
    return pl.pallas_call(
        _outproj_kernel,
        grid=(n // tm,),
        in_specs=[
            pl.BlockSpec((tm, D_MODEL), lambda i: (i, 0)),
            pl.BlockSpec((tm, A_DIM), lambda i: (i, 0)),
            pl.BlockSpec((tm, B_DIM), lambda i: (i, 0)),
            pl.BlockSpec((tm, C_DIM), lambda i: (i, 0)),
            pl.BlockSpec((tm, B_DIM), lambda i: (i, 3)),
            pl.BlockSpec((tm, C_DIM), lambda i: (i, 7)),
            pl.BlockSpec((1, D_MIX), const),
            pl.BlockSpec((A_DIM, A_DIM), const),
            pl.BlockSpec((D_MIX, D_MODEL), const),
        ],
        out_specs=pl.BlockSpec((tm, D_MODEL), lambda i: (i, 0)),
        out_shape=jax.ShapeDtypeStruct((n, D_MODEL), F32),
        compiler_params=pltpu.CompilerParams(dimension_semantics=("arbitrary",), vmem_limit_bytes=VMEM_LIMIT),
        name="outproj",
    )(x2, oa, ob, oc, rest, rest, out_g, grp, w_out)


def _ffn_kernel(x_ref, halo_ref, g_ref, wup_ref, cw_ref, cb_ref, wdn_ref, o_ref, h_ref, ug_ref, uu_ref, act_ref,
                *, tm, tiles_per_seq):
    i = pl.program_id(0)
    halo_rows = BF16_ROWS

    def rms(x):
        ms = jnp.mean(x * x, axis=-1, keepdims=True)
        return x * lax.rsqrt(ms + EPS) * g_ref[...]

    x = x_ref[...]
    first = (i % tiles_per_seq) == 0
    h_ref[0:halo_rows, :] = jnp.where(first, 0.0, rms(halo_ref[...])).astype(BF16)
    h_ref[halo_rows:, :] = rms(x).astype(BF16)
    h = h_ref[...]

    def conv(u_ref, slot, lo):
        y = cb_ref[:, lo:lo + FF_CHUNK] + cw_ref[FFN_CONV - 1:FFN_CONV, lo:lo + FF_CHUNK] * u_ref[slot, halo_rows:, :]
        for d in range(1, FFN_CONV):
            y = y + cw_ref[FFN_CONV - 1 - d:FFN_CONV - d, lo:lo + FF_CHUNK] * u_ref[slot, pl.ds(halo_rows - d, tm), :]
        return y

    def up_proj(c):
        lo = c * FF_CHUNK
        ug_ref[c % FF_SLOTS] = jnp.dot(h, wup_ref[:, lo:lo + FF_CHUNK], preferred_element_type=F32)
        uu_ref[c % FF_SLOTS] = jnp.dot(h, wup_ref[:, D_FF + lo:D_FF + lo + FF_CHUNK], preferred_element_type=F32)

    n_chunks = D_FF // FF_CHUNK
    ahead = FF_SLOTS - 1
    for c in range(min(ahead, n_chunks)):
        up_proj(c)
    for c in range(n_chunks):
        if c + ahead < n_chunks:
            up_proj(c + ahead)
        lo = c * FF_CHUNK
        gate = conv(ug_ref, c % FF_SLOTS, lo)
        up = conv(uu_ref, c % FF_SLOTS, D_FF + lo)
        act_ref[:, lo:lo + FF_CHUNK] = (gate * _sigmoid(gate) * up).astype(BF16)
    k1 = (n_chunks + 1) // 2 * FF_CHUNK
    y = x + jnp.dot(act_ref[:, 0:k1], wdn_ref[0:k1, :], preferred_element_type=F32)
    o_ref[...] = y + jnp.dot(act_ref[:, k1:D_FF], wdn_ref[k1:D_FF, :], preferred_element_type=F32)


def _ffn(x2, g, w_up, conv_w, conv_b, w_down, t):
    n = x2.shape[0]
    tm = TM_PROJ
    halo = BF16_ROWS
    const = lambda i: (0, 0)
    resident = pl.Buffered(1)
    return pl.pallas_call(
        functools.partial(_ffn_kernel, tm=tm, tiles_per_seq=t // tm),
        grid=(n // tm,),
        in_specs=[
            pl.BlockSpec((tm, D_MODEL), lambda i: (i, 0)),
            pl.BlockSpec((halo, D_MODEL), lambda i: (jnp.maximum(i * (tm // halo) - 1, 0), 0)),
            pl.BlockSpec((1, D_MODEL), const),
            pl.BlockSpec((D_MODEL, 2 * D_FF), const, pipeline_mode=resident),
            pl.BlockSpec((FFN_CONV, 2 * D_FF), const),
            pl.BlockSpec((1, 2 * D_FF), const),
            pl.BlockSpec((D_FF, D_MODEL), const, pipeline_mode=resident),
        ],
        out_specs=pl.BlockSpec((tm, D_MODEL), lambda i: (i, 0)),
        out_shape=jax.ShapeDtypeStruct((n, D_MODEL), F32),
        scratch_shapes=[
            pltpu.VMEM((tm + halo, D_MODEL), BF16),
            pltpu.VMEM((FF_SLOTS, tm + halo, FF_CHUNK), F32),
            pltpu.VMEM((FF_SLOTS, tm + halo, FF_CHUNK), F32),
            pltpu.VMEM((tm, D_FF), BF16),
        ],
        compiler_params=pltpu.CompilerParams(dimension_semantics=("arbitrary",), vmem_limit_bytes=VMEM_LIMIT),
        name="ffn",
    )(x2, x2, g, w_up, conv_w, conv_b, w_down)


def _block_diag_ones(n, blk, dtype):
    r = np.arange(n) // blk
    return jnp.asarray(r[:, None] == r[None, :], dtype)


def kernel(x, lb_logits, norm_mix_g, w_in, b_in, a_q_g, a_k_g, b_conv_w, out_g, w_out, norm_ffn_g, w_up,
           ffn_conv_w, ffn_conv_b, w_down):
    bsz, t, d = x.shape
    depth = w_in.shape[0]
    assert d == D_MODEL and t % TQ_FOX == 0 and t % TC_MLSTM == 0 and t % TC_HGRN == 0
    n = bsz * t
    x2 = x.reshape(n, d).astype(F32)

    p = jax.nn.softmax(lb_logits.astype(F32), axis=0)
    lb_all = jnp.maximum(jnp.cumsum(p, axis=0) - p[0], 0.0)
    log_lb = jnp.log(lb_all)
    log_1m_lb = jnp.log1p(-lb_all)

    grp_a = _block_diag_ones(A_DIM, HEAD_DIM, BF16)
    head_sel = jnp.asarray(np.arange(A_DIM)[:, None] // HEAD_DIM == np.arange(LANE)[None, :], BF16)
    grp_b = _block_diag_ones(B_DIM, HEAD_DIM, BF16)
    ar = np.arange(L_MLSTM)
    tri_u = jnp.asarray(ar[:, None] <= ar[None, :], BF16)
    tri_l = jnp.asarray(ar[:, None] >= ar[None, :], BF16)
    gate_row = np.arange(LANE)[:, None] - A_HEADS
    spread = jnp.asarray(gate_row == np.arange(2 * B_DIM)[None, :] // HEAD_DIM, BF16)
    ap = np.arange(TM_PROJ)
    tri_proj = jnp.asarray(ap[:, None] >= ap[None, :], BF16)
    ah = np.arange(TC_HGRN)
    same = (ah[:, None] // L_HGRN) == (ah[None, :] // L_HGRN)
    tri_bd = jnp.asarray(same & (ah[:, None] >= ah[None, :]), BF16)
    tot_bd = jnp.asarray(same, BF16)

    w_in_r = _reorder_in_cols(w_in).astype(BF16)
    for l in range(depth):
        w_r = w_in_r[l]
        b_r = _reorder_in_cols(b_in[l].astype(F32))[None, :]
        gq = (jnp.tile(a_q_g[l].astype(F32), A_HEADS) * (HEAD_DIM ** -0.5 * LOG2E))[None, :]
        gk = jnp.tile(a_k_g[l].astype(F32), A_HEADS)[None, :]
        bound = HEAD_DIM * jnp.max(jnp.abs(gq)) * jnp.max(jnp.abs(gk))
        qa, ka, va, rest, gates, c_edges = _inproj(x2, norm_mix_g[l][None, :].astype(F32), w_r, b_r, gq, gk, head_sel, tri_proj,
                                          jnp.full((1, LANE), bound, F32), t)
        oa = lax.cond(bound < FOX_BOUND_MAX, functools.partial(_fox, bsz=bsz, t=t),
                      functools.partial(_fox_safe, bsz=bsz, t=t), qa, ka, va, c_edges)
        ob = _mlstm(rest, gates, b_conv_w[l].astype(F32), tri_l, tri_u, grp_b, spread, bsz, t)
        oc = _hgrn(rest, log_lb[l][None, :], log_1m_lb[l][None, :], tri_bd, tot_bd, grp_b, bsz, t)
        x2 = _outproj(x2, oa, ob, oc, rest, out_g[l][None, :].astype(F32), grp_a, w_out[l].astype(BF16))
        x2 = _ffn(x2, norm_ffn_g[l][None, :].astype(F32), w_up[l].astype(BF16), ffn_conv_w[l].astype(F32),
                  ffn_conv_b[l][None, :].astype(F32), w_down[l].astype(BF16), t)
    return x2.reshape(bsz, t, d).astype(x.dtype)
```
